```python
import jax
import jax.numpy as jnp
from jax import lax
import numpy as np

D_MODEL = 2048
BATCH = 32
SEQ = 256
DEPTH = 4
DEC_BATCH = 2
DEC_SEQ = 2048
PAST_LEN = 256

GRID_W = 64
N_MIXERS = 3
N_MLA = (DEPTH + 2) // 3
N_HGRN = (DEPTH + 1) // 3
N_NAT = DEPTH // 3
N_MOD = 6
D_FF = 4 * D_MODEL
NORM_EPS = 1e-6
Q_BLOCK = 128
ROPE_THETA = 10000.0

MLA_HEADS = 16
MLA_NOPE = 128
MLA_ROPE = 64
MLA_QK = MLA_NOPE + MLA_ROPE
MLA_V = 128
MLA_Q_LORA = 768
MLA_KV_LORA = 512

HG_HEADS = 16
HG_K = 128
HG_V = D_MODEL // HG_HEADS
HG_CHUNK = 32

NAT_HEADS = 16
NAT_HD = D_MODEL // NAT_HEADS
NAT_WIN_ROWS = 8
NAT_WIN_COLS = 16

kernel_name = 'hybrid_mla_hgrn2_natten_diffusion_step'


def rms_norm(x, g):
    xf = x.astype(jnp.float32)
    y = xf * lax.rsqrt(jnp.mean(xf * xf, axis=-1, keepdims=True) + NORM_EPS)
    return (y * g.astype(jnp.float32)).astype(x.dtype)


def adaln(cvec, w, b):
    m = jax.nn.silu(cvec) @ w + b
    return [t[:, None, :] for t in jnp.split(m, N_MOD, axis=-1)]


def modulate(h, shift, scale):
    return h * (1 + scale) + shift


def sqrelu_mlp(h, w1, w2):
    return jnp.square(jax.nn.relu(h @ w1)) @ w2


def merge_heads(o):
    return o.reshape(o.shape[0], o.shape[1], -1)


def axial_rope(n_tok):
    n_freq = MLA_ROPE // 4
    inv = 1.0 / (ROPE_THETA ** (jnp.arange(n_freq, dtype=jnp.float32) / n_freq))
    t = jnp.arange(n_tok)
    row = (t // GRID_W).astype(jnp.float32)
    col = (t % GRID_W).astype(jnp.float32)
    ang = jnp.concatenate([row[:, None] * inv, col[:, None] * inv], axis=-1)
    return jnp.cos(ang)[None, :, None, :], jnp.sin(ang)[None, :, None, :]


def apply_rope(x, rope):
    cos, sin = rope
    x1, x2 = jnp.split(x.astype(jnp.float32), 2, axis=-1)
    return jnp.concatenate([x1 * cos - x2 * sin, x1 * sin + x2 * cos], axis=-1).astype(x.dtype)


def rope_tail(x, n_plain, rope):
    if rope is None:
        return x
    return jnp.concatenate([x[..., :n_plain], apply_rope(x[..., n_plain:], rope)], axis=-1)


def block_attention(q, k, v):
    b, nq, h, dq = q.shape
    nb = nq // Q_BLOCK
    scale = dq ** -0.5
    qb = jnp.moveaxis(q.reshape(b, nb, Q_BLOCK, h, dq), 1, 0)

    def attend(qi):
        s = jnp.einsum('bqhd,bkhd->bhqk', qi, k).astype(jnp.float32) * scale
        p = jax.nn.softmax(s, axis=-1).astype(v.dtype)
        return jnp.einsum('bhqk,bkhd->bqhd', p, v)

    o = lax.map(attend, qb)
    return jnp.moveaxis(o, 0, 1).reshape(b, nq, h, v.shape[-1])


def mla_queries(h, w_qa, g_qa, w_qb, g_qn, rope):
    b, n, _ = h.shape
    q = (rms_norm(h @ w_qa, g_qa) @ w_qb).reshape(b, n, MLA_HEADS, MLA_QK)
    return rope_tail(rms_norm(q, g_qn), MLA_NOPE, rope)


def mla_compress(h, w_kva, g_kva):
    kv = h @ w_kva
    return rms_norm(kv[..., :MLA_KV_LORA], g_kva), kv[..., MLA_KV_LORA:]


def mla_expand(ckv, kpe, w_kvb, g_kn, rope):
    b, n, _ = ckv.shape
    kv = (ckv @ w_kvb).reshape(b, n, MLA_HEADS, MLA_NOPE + MLA_V)
    k_pe = jnp.broadcast_to(kpe[:, :, None, :], (b, n, MLA_HEADS, MLA_ROPE))
    k = rms_norm(jnp.concatenate([kv[..., :MLA_NOPE], k_pe], axis=-1), g_kn)
    return rope_tail(k, MLA_NOPE, rope), kv[..., MLA_NOPE:]


def mla_mixer(hp, hs, ckv_cache, kpe_cache, rope, w_qa, g_qa, w_qb, g_qn, w_kva, g_kva, w_kvb, g_kn, w_o):
    ckv_p, kpe_p = mla_compress(hp, w_kva, g_kva)
    k_p, v_p = mla_expand(ckv_p, kpe_p, w_kvb, g_kn, None)
    o_p = block_attention(mla_queries(hp, w_qa, g_qa, w_qb, g_qn, None), k_p, v_p)
    ckv_s, kpe_s = mla_compress(hs, w_kva, g_kva)
    k_s, v_s = mla_expand(ckv_s, kpe_s, w_kvb, g_kn, rope)
    k_c, v_c = mla_expand(ckv_cache, kpe_cache, w_kvb, g_kn, None)
    o_s = block_attention(mla_queries(hs, w_qa, g_qa, w_qb, g_qn, rope),
                          jnp.concatenate([k_s, k_c], axis=1), jnp.concatenate([v_s, v_c], axis=1))
    return merge_heads(o_p) @ w_o, merge_heads(o_s) @ w_o, ckv_p, kpe_p


def hgrn_chunk_scan(q, k, v, log_f, s0):
    b, n, h, dk = q.shape
    dv = v.shape[-1]
    nc = n // HG_CHUNK

    def chunks(a):
        return a.reshape(b, nc, HG_CHUNK, h, a.shape[-1]).transpose(1, 0, 3, 2, 4)

    tri = jnp.tril(jnp.ones((HG_CHUNK, HG_CHUNK), dtype=bool))[:, :, None]

    def step(S, inp):
        qc, kc, vc, gc = inp
        cum = jnp.cumsum(gc, axis=2)
        rel = jnp.exp(jnp.where(tri, cum[:, :, :, None, :] - cum[:, :, None, :, :], -jnp.inf))
        att = jnp.einsum('bhtk,bhsk,bhtsk->bhts', qc, kc, rel)
        o = jnp.einsum('bhts,bhsv->bhtv', att, vc) + jnp.einsum('bhtk,bhkv->bhtv', qc * jnp.exp(cum), S)
        last = cum[:, :, -1, :]
        S = S * jnp.exp(last)[..., None] + jnp.einsum('bhsk,bhsv->bhkv', kc * jnp.exp(last[:, :, None, :] - cum), vc)
        return S, o

    S, o = lax.scan(step, s0, tuple(chunks(a) for a in (q, k, v, log_f)))
    return o.transpose(1, 0, 3, 2, 4).reshape(b, n, h, dv), S


def hgrn_mixer(h, s0, lb, w_q, w_f, w_i, w_g, g_o, w_o):
    b, n, _ = h.shape
    q = jax.nn.silu((h @ w_q).astype(jnp.float32)).reshape(b, n, HG_HEADS, HG_K)
    v = (h @ w_i).astype(jnp.float32).reshape(b, n, HG_HEADS, HG_V)
    outs, states = [], []
    for d in range(2):
        f = lb[d] + (1.0 - lb[d]) * jax.nn.sigmoid((h @ w_f[d]).astype(jnp.float32))
        f = f.reshape(b, n, HG_HEADS, HG_K)
        seq = (q, 1.0 - f, v, jnp.log(f))
        if d == 1:
            seq = tuple(jnp.flip(a, axis=1) for a in seq)
        o, s_fin = hgrn_chunk_scan(*seq, s0[:, d].astype(jnp.float32))
        outs.append(jnp.flip(o, axis=1) if d == 1 else o)
        states.append(s_fin)
    o = rms_norm(outs[0] + outs[1], g_o).reshape(b, n, D_MODEL).astype(h.dtype)
    o = o * jax.nn.silu(h @ w_g)
    return o @ w_o, jnp.stack(states, axis=1)


def nat_qkv(h, w_qkv, g_q, g_k):
    b, n, _ = h.shape
    qkv = (h @ w_qkv).reshape(b, n, 3, NAT_HEADS, NAT_HD)
    return rms_norm(qkv[:, :, 0], g_q), rms_norm(qkv[:, :, 1], g_k), qkv[:, :, 2]


def neighbourhood_attention(q, k, v, k_ctx, v_ctx, rpb):
    b, n, h, d = q.shape
    rows = n // GRID_W
    wr = min(NAT_WIN_ROWS, rows)
    nw = wr * NAT_WIN_COLS
    scale = d ** -0.5
    qg = q.reshape(b, rows, GRID_W, h, d)
    kg = k.reshape(b, rows, GRID_W, h, d)
    vg = v.reshape(b, rows, GRID_W, h, d)
    col = jnp.arange(GRID_W)
    col_idx = jnp.clip(col - NAT_WIN_COLS // 2, 0, GRID_W - NAT_WIN_COLS)[:, None] + jnp.arange(NAT_WIN_COLS)[None, :]
    dc = col_idx - col[:, None] + NAT_WIN_COLS - 1

    def one_row(args):
        r, q_r = args
        r0 = jnp.clip(r - wr // 2, 0, rows - wr)
        k_win = lax.dynamic_slice_in_dim(kg, r0, wr, axis=1)[:, :, col_idx]
        v_win = lax.dynamic_slice_in_dim(vg, r0, wr, axis=1)[:, :, col_idx]
        dr = r0 + jnp.arange(wr) - r + NAT_WIN_ROWS - 1
        bias = jnp.transpose(rpb[:, dr[:, None, None], dc[None, :, :]], (0, 2, 1, 3))
        s_win = jnp.einsum('bqhd,brqchd->bhqrc', q_r, k_win).astype(jnp.float32) * scale + bias.astype(jnp.float32)
        s_ctx = jnp.einsum('bqhd,bkhd->bhqk', q_r, k_ctx).astype(jnp.float32) * scale
        p = jax.nn.softmax(jnp.concatenate([s_win.reshape(b, h, GRID_W, nw), s_ctx], axis=-1), axis=-1).astype(v.dtype)
        p_win = p[..., :nw].reshape(b, h, GRID_W, wr, NAT_WIN_COLS)
        return jnp.einsum('bhqrc,brqchd->bqhd', p_win, v_win) + jnp.einsum('bhqk,bkhd->bqhd', p[..., nw:], v_ctx)

    o = lax.map(one_row, (jnp.arange(rows), jnp.moveaxis(qg, 1, 0)))
    return jnp.moveaxis(o, 0, 1).reshape(b, n, h, d)


def nat_mixer(hp, hs, k_cache, v_cache, w_qkv, g_q, g_k, rpb, w_o):
    q_p, k_p, v_p = nat_qkv(hp, w_qkv, g_q, g_k)
    o_p = block_attention(q_p, k_p, v_p)
    q_s, k_s, v_s = nat_qkv(hs, w_qkv, g_q, g_k)
    o_s = neighbourhood_attention(q_s, k_s, v_s, k_cache, v_cache, rpb)
    return merge_heads(o_p) @ w_o, merge_heads(o_s) @ w_o, k_p, v_p


def setup_inputs(seed: int = 0) -> dict:
    key = jax.random.key(seed)
    ks = iter(jax.random.split(key, 40))

    def nrm(shape, s=1.0):
        return jax.random.normal(next(ks), shape, jnp.float32) * s

    def lin(shape, g=1.0):
        return nrm(shape, g * shape[-2] ** -0.5)

    def gain(shape):
        return 1.0 + nrm(shape, 0.01)

    D = D_MODEL
    return {
        'x_prompt': nrm((BATCH, SEQ, D)),
        'x_sample': nrm((DEC_BATCH, DEC_SEQ, D)),
        'cache_mla_ckv': nrm((DEC_BATCH, N_MLA, PAST_LEN, MLA_KV_LORA)),
        'cache_mla_kpe': nrm((DEC_BATCH, N_MLA, PAST_LEN, MLA_ROPE)),
        'state_hgrn': nrm((DEC_BATCH, N_HGRN, 2, HG_HEADS, HG_K, HG_V), 0.5),
        'cache_nat_k': nrm((DEC_BATCH, N_NAT, PAST_LEN, NAT_HEADS, NAT_HD)),
        'cache_nat_v': nrm((DEC_BATCH, N_NAT, PAST_LEN, NAT_HEADS, NAT_HD)),
        'c': nrm((DEC_BATCH, D)),
        'c_ctx': nrm((D,)),
        'w_mod': lin((DEPTH, D, N_MOD * D), 0.5),
        'b_mod': nrm((DEPTH, N_MOD * D), 0.01),
        'g_norm_mix': gain((DEPTH, D)),
        'g_norm_ffn': gain((DEPTH, D)),
        'w_mla_qa': lin((N_MLA, D, MLA_Q_LORA)),
        'g_mla_qa': gain((N_MLA, MLA_Q_LORA)),
        'w_mla_qb': lin((N_MLA, MLA_Q_LORA, MLA_HEADS * MLA_QK)),
        'g_mla_qn': gain((N_MLA, MLA_QK)),
        'w_mla_kva': lin((N_MLA, D, MLA_KV_LORA + MLA_ROPE)),
        'g_mla_kva': gain((N_MLA, MLA_KV_LORA)),
        'w_mla_kvb': lin((N_MLA, MLA_KV_LORA, MLA_HEADS * (MLA_NOPE + MLA_V))),
        'g_mla_kn': gain((N_MLA, MLA_QK)),
        'w_mla_o': lin((N_MLA, MLA_HEADS * MLA_V, D)),
        'w_hg_q': lin((N_HGRN, D, HG_HEADS * HG_K)),
        'w_hg_f': lin((N_HGRN, 2, D, HG_HEADS * HG_K)),
        'hg_lower_bounds': nrm((2, DEPTH, HG_HEADS * HG_K)),
        'w_hg_i': lin((N_HGRN, D, HG_HEADS * HG_V)),
        'w_hg_g': lin((N_HGRN, D, D)),
        'g_hg_o': gain((N_HGRN, HG_V)),
        'w_hg_o': lin((N_HGRN, D, D)),
        'w_nat_qkv': lin((N_NAT, D, 3 * D)),
        'g_nat_q': gain((N_NAT, NAT_HD)),
        'g_nat_k': gain((N_NAT, NAT_HD)),
        'nat_rpb': nrm((N_NAT, NAT_HEADS, 2 * NAT_WIN_ROWS - 1, 2 * NAT_WIN_COLS - 1), 0.5),
        'w_nat_o': lin((N_NAT, D, D)),
        'w_ff1': lin((DEPTH, D, D_FF)),
        'w_ff2': lin((DEPTH, D_FF, D)),
    }


def reference(x_prompt, x_sample, cache_mla_ckv, cache_mla_kpe, state_hgrn, cache_nat_k, cache_nat_v, c, c_ctx,
              w_mod, b_mod, g_norm_mix, g_norm_ffn,
              w_mla_qa, g_mla_qa, w_mla_qb, g_mla_qn, w_mla_kva, g_mla_kva, w_mla_kvb, g_mla_kn, w_mla_o,
              w_hg_q, w_hg_f, hg_lower_bounds, w_hg_i, w_hg_g, g_hg_o, w_hg_o,
              w_nat_qkv, g_nat_q, g_nat_k, nat_rpb, w_nat_o,
              w_ff1, w_ff2):
    b_p = x_prompt.shape[0]
    rope = axial_rope(x_sample.shape[1])
    lb_all = jnp.cumsum(jax.nn.softmax(hg_lower_bounds.astype(jnp.float32), axis=1), axis=1)
    lb_all = lb_all - lb_all[:, :1]
    xp, xs = x_prompt, x_sample
    new_ckv, new_kpe, new_hg, new_nat_k, new_nat_v = [], [], [], [], []
    for i in range(DEPTH):
        kind, j = i % N_MIXERS, i // N_MIXERS
        mod_p = adaln(c_ctx[None, :], w_mod[i], b_mod[i])
        mod_s = adaln(c, w_mod[i], b_mod[i])
        hp = modulate(rms_norm(xp, g_norm_mix[i]), mod_p[0], mod_p[1])
        hs = modulate(rms_norm(xs, g_norm_mix[i]), mod_s[0], mod_s[1])
        if kind == 0:
            dp, ds, ckv_p, kpe_p = mla_mixer(hp, hs, cache_mla_ckv[:, j], cache_mla_kpe[:, j], rope,
                                             w_mla_qa[j], g_mla_qa[j], w_mla_qb[j], g_mla_qn[j],
                                             w_mla_kva[j], g_mla_kva[j], w_mla_kvb[j], g_mla_kn[j], w_mla_o[j])
            new_ckv.append(ckv_p)
            new_kpe.append(kpe_p)
        elif kind == 1:
            zero_state = jnp.zeros((b_p, 2, HG_HEADS, HG_K, HG_V), jnp.float32)
            dp, s_ctx = hgrn_mixer(hp, zero_state, lb_all[:, i], w_hg_q[j], w_hg_f[j], w_hg_i[j], w_hg_g[j], g_hg_o[j], w_hg_o[j])
            ds, _ = hgrn_mixer(hs, state_hgrn[:, j], lb_all[:, i], w_hg_q[j], w_hg_f[j], w_hg_i[j], w_hg_g[j], g_hg_o[j], w_hg_o[j])
            new_hg.append(s_ctx.astype(x_prompt.dtype))
        else:
            dp, ds, k_p, v_p = nat_mixer(hp, hs, cache_nat_k[:, j], cache_nat_v[:, j],
                                         w_nat_qkv[j], g_nat_q[j], g_nat_k[j], nat_rpb[j], w_nat_o[j])
            new_nat_k.append(k_p)
            new_nat_v.append(v_p)
        xp = xp + mod_p[2] * dp
        xs = xs + mod_s[2] * ds
        xp = xp + mod_p[5] * sqrelu_mlp(modulate(rms_norm(xp, g_norm_ffn[i]), mod_p[3], mod_p[4]), w_ff1[i], w_ff2[i])
        xs = xs + mod_s[5] * sqrelu_mlp(modulate(rms_norm(xs, g_norm_ffn[i]), mod_s[3], mod_s[4]), w_ff1[i], w_ff2[i])
    return (xp, xs, jnp.stack(new_ckv, axis=1), jnp.stack(new_kpe, axis=1), jnp.stack(new_hg, axis=1),
            jnp.stack(new_nat_k, axis=1), jnp.stack(new_nat_v, axis=1))
```

```python
import functools
import math

import jax
import jax.numpy as jnp
import numpy as np
from jax import lax
from jax.experimental import pallas as pl
from jax.experimental.pallas import tpu as pltpu

F32 = jnp.float32
BF16 = jnp.bfloat16
EPS = 1e-6
ROPE_THETA = 10000.0
V7X_VMEM_LIMIT_BYTES = 56 * 1024 * 1024
LANES = 128

N_MOD = 6
GRID_W = 64
MLA_HEADS, MLA_NOPE, MLA_ROPE, MLA_V = 16, 128, 64, 128
MLA_QK = MLA_NOPE + MLA_ROPE
MLA_QK_PAD = 256
MLA_Q_LORA, MLA_KV_LORA = 768, 512
HG_HEADS, HG_K, HG_V = 16, 128, 128
HG_TILE = 128
NAT_HEADS, NAT_HD = 16, 128
NAT_WIN_ROWS, NAT_WIN_COLS = 8, 16
NAT_Q_ROWS = 4
NAT_K_ROWS = 12
NEG_BIG = -1e30


def _params(sem):
    return pltpu.CompilerParams(dimension_semantics=sem, vmem_limit_bytes=V7X_VMEM_LIMIT_BYTES)


def _mod_row(i, tm, mp, ns):
    start = i * tm
    return jnp.where(start < mp, 0, 1 + (start - mp) // ns)


def _norm_modulate(x, g, shift, scale):
    y = x * lax.rsqrt(jnp.mean(x * x, axis=-1, keepdims=True) + EPS) * g
    return y * (1.0 + scale) + shift


def _adaln_kernel(c_ref, w_ref, b_ref, o_ref):
    c = c_ref[...]
    s = (c * jax.nn.sigmoid(c)).astype(BF16)
    o_ref[0] = jnp.dot(s, w_ref[0].astype(BF16), preferred_element_type=F32) + b_ref[0]


def adaln_all(cvec, w_mod, b_mod):
    depth, d, n = w_mod.shape
    rows = cvec.shape[0]
    tn = 1024
    return pl.pallas_call(
        _adaln_kernel,
        grid=(depth, n // tn),
        in_specs=[
            pl.BlockSpec((rows, d), lambda l, j: (0, 0)),
            pl.BlockSpec((1, d, tn), lambda l, j: (l, 0, j)),
            pl.BlockSpec((1, 1, tn), lambda l, j: (l, 0, j)),
        ],
        out_specs=pl.BlockSpec((1, rows, tn), lambda l, j: (l, 0, j)),
        out_shape=jax.ShapeDtypeStruct((depth, rows, n), F32),
        compiler_params=_params(("parallel", "parallel")),
        name="adaln",
    )(cvec, w_mod, b_mod.reshape(depth, 1, n))


def _nm_matmul_kernel(x_ref, g_ref, sh_ref, sc_ref, w_ref, gain_ref, o_ref, h_scr, *, n_norm_tiles, head):
    j = pl.program_id(1)

    @pl.when(j == 0)
    def _():
        h_scr[...] = _norm_modulate(x_ref[...], g_ref[...], sh_ref[0], sc_ref[0]).astype(BF16)

    acc = jnp.dot(h_scr[...], w_ref[...], preferred_element_type=F32)
    if n_norm_tiles == 0:
        o_ref[...] = acc
    else:

        @pl.when(j < n_norm_tiles)
        def _():
            for c in range(acc.shape[1] // head):
                blk = acc[:, c * head:(c + 1) * head]
                r = lax.rsqrt(jnp.mean(blk * blk, axis=-1, keepdims=True) + EPS)
                o_ref[:, c * head:(c + 1) * head] = blk * r * gain_ref[:, c * head:(c + 1) * head]

        @pl.when(j >= n_norm_tiles)
        def _():
            o_ref[...] = acc


def nm_matmul(x, g, modrows, k_shift, k_scale, w, mp, ns, *, tm, tn, gains=None, n_norm_cols=0, head=LANES):
    m, d = x.shape
    n = w.shape[1]
    if gains is None:
        gains = jnp.ones((1, n), F32)
    row = functools.partial(_mod_row, tm=tm, mp=mp, ns=ns)
    return pl.pallas_call(
        functools.partial(_nm_matmul_kernel, n_norm_tiles=n_norm_cols // tn, head=head),
        grid=(m // tm, n // tn),
        in_specs=[
            pl.BlockSpec((tm, d), lambda i, j: (i, 0)),
            pl.BlockSpec((1, d), lambda i, j: (0, 0)),
            pl.BlockSpec((1, 1, d), lambda i, j: (row(i) * N_MOD + k_shift, 0, 0)),
            pl.BlockSpec((1, 1, d), lambda i, j: (row(i) * N_MOD + k_scale, 0, 0)),
            pl.BlockSpec((d, tn), lambda i, j: (0, j)),
            pl.BlockSpec((1, tn), lambda i, j: (0, j)),
        ],
        out_specs=pl.BlockSpec((tm, tn), lambda i, j: (i, j)),
        out_shape=jax.ShapeDtypeStruct((m, n), F32),
        scratch_shapes=[pltpu.VMEM((tm, d), BF16)],
        compiler_params=_params(("parallel", "arbitrary")),
        name="nm_matmul",
    )(x, g.reshape(1, d), modrows, modrows, w, gains)


def _matmul_res_kernel(a_ref, w_ref, x_ref, gt_ref, o_ref):
    acc = jnp.dot(a_ref[...], w_ref[...], preferred_element_type=F32)
    o_ref[...] = x_ref[...] + gt_ref[0] * acc


def matmul_res(a, w, x, modrows, k_gate, mp, ns, *, tm):
    m, kdim = a.shape
    n = w.shape[1]
    row = functools.partial(_mod_row, tm=tm, mp=mp, ns=ns)
    return pl.pallas_call(
        _matmul_res_kernel,
        grid=(m // tm,),
        in_specs=[
            pl.BlockSpec((tm, kdim), lambda i: (i, 0)),
            pl.BlockSpec((kdim, n), lambda i: (0, 0)),
            pl.BlockSpec((tm, n), lambda i: (i, 0)),
            pl.BlockSpec((1, 1, n), lambda i: (row(i) * N_MOD + k_gate, 0, 0)),
        ],
        out_specs=pl.BlockSpec((tm, n), lambda i: (i, 0)),
        out_shape=jax.ShapeDtypeStruct((m, n), F32),
        compiler_params=_params(("parallel",)),
        name="matmul_res",
    )(a, w, x, modrows)


def _mlp_kernel(x_ref, g_ref, sh_ref, sc_ref, gt_ref, w1_ref, w2_ref, o_ref, h_scr, acc_scr):
    f = pl.program_id(1)

    @pl.when(f == 0)
    def _():
        h_scr[...] = _norm_modulate(x_ref[...], g_ref[...], sh_ref[0], sc_ref[0]).astype(BF16)
        acc_scr[...] = jnp.zeros_like(acc_scr)

    u = jnp.dot(h_scr[...], w1_ref[...], preferred_element_type=F32)
    u = jnp.square(jnp.maximum(u, 0.0)).astype(BF16)
    acc_scr[...] += jnp.dot(u, w2_ref[...], preferred_element_type=F32)

    @pl.when(f == pl.num_programs(1) - 1)
    def _():
        o_ref[...] = x_ref[...] + gt_ref[0] * acc_scr[...]


def fused_mlp(x, g, modrows, w1, w2, mp, ns, *, tm, tf):
    m, d = x.shape
    ff = w1.shape[1]
    row = functools.partial(_mod_row, tm=tm, mp=mp, ns=ns)
    modspec = lambda k: pl.BlockSpec((1, 1, d), lambda i, f: (row(i) * N_MOD + k, 0, 0))
    return pl.pallas_call(
        _mlp_kernel,
        grid=(m // tm, ff // tf),
        in_specs=[
            pl.BlockSpec((tm, d), lambda i, f: (i, 0)),
            pl.BlockSpec((1, d), lambda i, f: (0, 0)),
            modspec(3), modspec(4), modspec(5),
            pl.BlockSpec((d, tf), lambda i, f: (0, f)),
            pl.BlockSpec((tf, d), lambda i, f: (f, 0)),
        ],
        out_specs=pl.BlockSpec((tm, d), lambda i, f: (i, 0)),
        out_shape=jax.ShapeDtypeStruct((m, d), F32),
        scratch_shapes=[pltpu.VMEM((tm, d), BF16), pltpu.VMEM((tm, d), F32)],
        compiler_params=_params(("parallel", "arbitrary")),
        name="fused_mlp",
    )(x, g.reshape(1, d), modrows, modrows, modrows, w1, w2)


def _rope_lanes(x, c_ref, s1_ref, s2_ref):
    return x * c_ref[...] + pltpu.roll(x, 96, 1) * s1_ref[...] + pltpu.roll(x, 32, 1) * s2_ref[...]


def _mla_q_kernel(qa_ref, gqa_ref, w_ref, gqn_ref, c_ref, s1_ref, s2_ref, o_ref):
    qa = qa_ref[...]
    qa = (qa * lax.rsqrt(jnp.mean(qa * qa, axis=-1, keepdims=True) + EPS) * gqa_ref[...]).astype(BF16)
    q = jnp.dot(qa, w_ref[...], preferred_element_type=F32)
    for h in range(MLA_HEADS):
        qh = q[:, h * MLA_QK_PAD:(h + 1) * MLA_QK_PAD]
        r = lax.rsqrt(jnp.sum(qh * qh, axis=-1, keepdims=True) / MLA_QK + EPS)
        qn = qh * r * gqn_ref[...]
        o_ref[:, h * MLA_QK_PAD:h * MLA_QK_PAD + LANES] = qn[:, :LANES].astype(BF16)
        o_ref[:, h * MLA_QK_PAD + LANES:(h + 1) * MLA_QK_PAD] = _rope_lanes(
            qn[:, LANES:], c_ref, s1_ref, s2_ref).astype(BF16)


def _rope_spec(tm, mp, ns):
    def idx(i):
        start = i * tm
        return (jnp.where(start < mp, 0, 1 + ((start - mp) % ns) // tm), 0)

    return pl.BlockSpec((tm, LANES), idx)


def mla_q(y1, g_qa, w_qb_pad, g_qn_pad, tables, mp, ns, *, tm):
    m = y1.shape[0]
    n = MLA_HEADS * MLA_QK_PAD
    rs = _rope_spec(tm, mp, ns)
    return pl.pallas_call(
        _mla_q_kernel,
        grid=(m // tm,),
        in_specs=[
            pl.BlockSpec((tm, MLA_Q_LORA), lambda i: (i, 1)),
            pl.BlockSpec((1, MLA_Q_LORA), lambda i: (0, 0)),
            pl.BlockSpec((MLA_Q_LORA, n), lambda i: (0, 0)),
            pl.BlockSpec((1, MLA_QK_PAD), lambda i: (0, 0)),
            rs, rs, rs,
        ],
        out_specs=pl.BlockSpec((tm, n), lambda i: (i, 0)),
        out_shape=jax.ShapeDtypeStruct((m, n), BF16),
        compiler_params=_params(("parallel",)),
        name="mla_q",
    )(y1, g_qa.reshape(1, MLA_Q_LORA), w_qb_pad, g_qn_pad, *tables)


def _mla_kv_kernel(ckv_ref, kpe_ref, gkva_ref, w_ref, gkn_ref, c_ref, s1_ref, s2_ref,
                   ckv_out_ref, k_ref, v_ref, *, normalize):
    ckv = ckv_ref[...]
    if normalize:
        ckv = ckv * lax.rsqrt(jnp.mean(ckv * ckv, axis=-1, keepdims=True) + EPS) * gkva_ref[...]
    ckv_out_ref[...] = ckv
    kv = jnp.dot(ckv.astype(BF16), w_ref[...], preferred_element_type=F32)
    kpe = kpe_ref[...]
    pe_ss = jnp.sum(kpe * kpe, axis=-1, keepdims=True)
    for h in range(MLA_HEADS):
        nope = kv[:, h * 2 * LANES:h * 2 * LANES + LANES]
        r = lax.rsqrt((jnp.sum(nope * nope, axis=-1, keepdims=True) + pe_ss) / MLA_QK + EPS)
        k_ref[:, h * MLA_QK_PAD:h * MLA_QK_PAD + LANES] = (nope * r * gkn_ref[:, :LANES]).astype(BF16)
        k_ref[:, h * MLA_QK_PAD + LANES:(h + 1) * MLA_QK_PAD] = _rope_lanes(
            kpe * r * gkn_ref[:, LANES:], c_ref, s1_ref, s2_ref).astype(BF16)
        v_ref[:, h * MLA_V:(h + 1) * MLA_V] = kv[:, h * 2 * LANES + LANES:(h + 1) * 2 * LANES].astype(BF16)


def mla_kv(ckv_src, ckv_col, kpe_src, kpe_col, g_kva, w_kvb, g_kn_pad, tables, mp, ns, *, tm, normalize):
    m = ckv_src.shape[0]
    rs = _rope_spec(tm, mp, ns)
    return pl.pallas_call(
        functools.partial(_mla_kv_kernel, normalize=normalize),
        grid=(m // tm,),
        in_specs=[
            pl.BlockSpec((tm, MLA_KV_LORA), lambda i: (i, ckv_col)),
            pl.BlockSpec((tm, LANES), lambda i: (i, kpe_col)),
            pl.BlockSpec((1, MLA_KV_LORA), lambda i: (0, 0)),
            pl.BlockSpec((MLA_KV_LORA, MLA_HEADS * 2 * LANES), lambda i: (0, 0)),
            pl.BlockSpec((1, MLA_QK_PAD), lambda i: (0, 0)),
            rs, rs, rs,
        ],
        out_specs=[
            pl.BlockSpec((tm, MLA_KV_LORA), lambda i: (i, 0)),
            pl.BlockSpec((tm, MLA_HEADS * MLA_QK_PAD), lambda i: (i, 0)),
            pl.BlockSpec((tm, MLA_HEADS * MLA_V), lambda i: (i, 0)),
        ],
        out_shape=[
            jax.ShapeDtypeStruct((m, MLA_KV_LORA), F32),
            jax.ShapeDtypeStruct((m, MLA_HEADS * MLA_QK_PAD), BF16),
            jax.ShapeDtypeStruct((m, MLA_HEADS * MLA_V), BF16),
        ],
        compiler_params=_params(("parallel",)),
        name="mla_kv",
    )(ckv_src, kpe_src, g_kva.reshape(1, MLA_KV_LORA), w_kvb, g_kn_pad, *tables)


def _rope_tables(tm, ns):
    n_freq = MLA_ROPE // 4
    inv = 1.0 / (ROPE_THETA ** (jnp.arange(n_freq, dtype=F32) / n_freq))
    t = jnp.arange(ns)
    row = (t // GRID_W).astype(F32)
    col = (t % GRID_W).astype(F32)
    ang = jnp.concatenate([row[:, None] * inv, col[:, None] * inv], axis=-1)
    cos, sin = jnp.cos(ang), jnp.sin(ang)
    half = MLA_ROPE // 2
    z = jnp.zeros((ns, LANES - 2 * half), F32)
    zh = jnp.zeros((ns, half), F32)
    c = jnp.concatenate([cos, cos, z], axis=-1)
    s1 = jnp.concatenate([-sin, zh, z], axis=-1)
    s2 = jnp.concatenate([zh, sin, z], axis=-1)
    ident = jnp.concatenate([jnp.ones((tm, 2 * half), F32), jnp.zeros((tm, LANES - 2 * half), F32)], axis=-1)
    zero = jnp.zeros((tm, LANES), F32)
    return (jnp.concatenate([ident, c], axis=0), jnp.concatenate([zero, s1], axis=0),
            jnp.concatenate([zero, s2], axis=0))


def _softmax_parts(scores):
    m = scores[0].max(axis=-1, keepdims=True)
    for s in scores[1:]:
        m = jnp.maximum(m, s.max(axis=-1, keepdims=True))
    ps = [jnp.exp(s - m) for s in scores]
    l = ps[0].sum(axis=-1, keepdims=True)
    for p in ps[1:]:
        l = l + p.sum(axis=-1, keepdims=True)
    return ps, l


def _qkt(q, k):
    return lax.dot_general(q, k, (((1,), (1,)), ((), ())), preferred_element_type=F32)


def _attn_block_kernel(q_ref, k_ref, v_ref, o_ref, *, heads, dq, dv, scale):
    for h in range(heads):
        q = q_ref[:, h * dq:(h + 1) * dq].astype(BF16)
        k = k_ref[:, h * dq:(h + 1) * dq].astype(BF16)
        v = v_ref[:, h * dv:(h + 1) * dv].astype(BF16)
        (p,), l = _softmax_parts([_qkt(q, k) * scale])
        o = jnp.dot(p.astype(BF16), v, preferred_element_type=F32) / l
        o_ref[:, h * dv:(h + 1) * dv] = o.astype(o_ref.dtype)


def attn_block(q_src, q_col, k_src, k_col, v_src, v_col, nb, seq, *, heads, dq, dv, scale):
    return pl.pallas_call(
        functools.partial(_attn_block_kernel, heads=heads, dq=dq, dv=dv, scale=scale),
        grid=(nb,),
        in_specs=[
            pl.BlockSpec((seq, heads * dq), lambda b: (b, q_col)),
            pl.BlockSpec((seq, heads * dq), lambda b: (b, k_col)),
            pl.BlockSpec((seq, heads * dv), lambda b: (b, v_col)),
        ],
        out_specs=pl.BlockSpec((seq, heads * dv), lambda b: (b, 0)),
        out_shape=jax.ShapeDtypeStruct((nb * seq, heads * dv), BF16),
        compiler_params=_params(("parallel",)),
        name="attn_block",
    )(q_src, k_src, v_src)


def _attn_latent_kernel(q_ref, k_ref, v_ref, kc_ref, vc_ref, o_ref, *, scale):
    q = q_ref[...]
    (p1, p2), l = _softmax_parts([_qkt(q, k_ref[...]) * scale, _qkt(q, kc_ref[...]) * scale])
    o = jnp.dot(p1.astype(BF16), v_ref[...], preferred_element_type=F32)
    o = o + jnp.dot(p2.astype(BF16), vc_ref[...], preferred_element_type=F32)
    o_ref[...] = (o / l).astype(o_ref.dtype)


def attn_latent(q2d, k2d, v2d, kc2d, vc2d, mp, ns, nbs, past, *, heads, dq, dv, scale, tq):
    return pl.pallas_call(
        functools.partial(_attn_latent_kernel, scale=scale),
        grid=(nbs, heads, ns // tq),
        in_specs=[
            pl.BlockSpec((tq, dq), lambda b, h, i: ((mp + b * ns) // tq + i, h)),
            pl.BlockSpec((ns, dq), lambda b, h, i: (mp // ns + b, h)),
            pl.BlockSpec((ns, dv), lambda b, h, i: (mp // ns + b, h)),
            pl.BlockSpec((past, dq), lambda b, h, i: (b, h)),
            pl.BlockSpec((past, dv), lambda b, h, i: (b, h)),
        ],
        out_specs=pl.BlockSpec((tq, dv), lambda b, h, i: (b * (ns // tq) + i, h)),
        out_shape=jax.ShapeDtypeStruct((nbs * ns, heads * dv), BF16),
        compiler_params=_params(("parallel", "parallel", "arbitrary")),
        name="attn_latent",
    )(q2d, k2d, v2d, kc2d, vc2d)


def _nat_key_row0(i, rows):
    return jnp.clip(NAT_Q_ROWS * i - NAT_WIN_ROWS // 2, 0, rows - NAT_K_ROWS)


def _nat_kernel(q_ref, k_ref, v_ref, kc_ref, vc_ref, b_ref, o_ref, *, scale, rows):
    i = pl.program_id(2)
    start = pl.multiple_of(_nat_key_row0(i, rows) * GRID_W, GRID_W)
    nk = NAT_K_ROWS * GRID_W
    q = q_ref[...].astype(BF16)
    kw = k_ref[pl.ds(start, nk), :].astype(BF16)
    vw = v_ref[pl.ds(start, nk), :].astype(BF16)
    s_win = _qkt(q, kw) * scale + b_ref[0, 0]
    s_ctx = _qkt(q, kc_ref[...].astype(BF16)) * scale
    (p1, p2), l = _softmax_parts([s_win, s_ctx])
    o = jnp.dot(p1.astype(BF16), vw, preferred_element_type=F32)
    o = o + jnp.dot(p2.astype(BF16), vc_ref[...].astype(BF16), preferred_element_type=F32)
    o_ref[...] = (o / l).astype(o_ref.dtype)


def _nat_bias_variants(rows):
    return (0, 1, rows // NAT_Q_ROWS - 1)


def _nat_bias(rpb, rows):
    w = GRID_W
    a = np.arange(NAT_Q_ROWS)[:, None, None, None]
    qc = np.arange(w)[None, :, None, None]
    kl = np.arange(NAT_K_ROWS)[None, None, :, None]
    kc = np.arange(w)[None, None, None, :]
    drs, dcs, valids = [], [], []
    for i in _nat_bias_variants(rows):
        row0 = min(max(NAT_Q_ROWS * i - NAT_WIN_ROWS // 2, 0), rows - NAT_K_ROWS)
        qr = NAT_Q_ROWS * i + a
        r0 = np.clip(qr - NAT_WIN_ROWS // 2, 0, rows - NAT_WIN_ROWS)
        kr = row0 + kl
        c0 = np.clip(qc - NAT_WIN_COLS // 2, 0, w - NAT_WIN_COLS)
        valid = (kr >= r0) & (kr < r0 + NAT_WIN_ROWS) & (kc >= c0) & (kc < c0 + NAT_WIN_COLS)
        dr = np.clip(kr - qr + NAT_WIN_ROWS - 1, 0, 2 * NAT_WIN_ROWS - 2)
        dc = np.clip(kc - qc + NAT_WIN_COLS - 1, 0, 2 * NAT_WIN_COLS - 2)
        shape = (NAT_Q_ROWS * w, NAT_K_ROWS * w)
        drs.append(np.broadcast_to(dr, valid.shape).reshape(shape))
        dcs.append(np.broadcast_to(dc, valid.shape).reshape(shape))
        valids.append(valid.reshape(shape))
    dr, dc, valid = np.stack(drs), np.stack(dcs), np.stack(valids)
    return jnp.where(valid[None], rpb[:, dr, dc], NEG_BIG).astype(F32)


def nat_latent(y, kc2d, vc2d, bias, mp, ns, nbs, past, *, scale):
    rows = ns // GRID_W
    tq = NAT_Q_ROWS * GRID_W
    nblk = ns // tq
    last = nblk - 1

    def bias_idx(b, h, i):
        return (h, jnp.where(i == 0, 0, jnp.where(i == last, 2, 1)), 0, 0)

    return pl.pallas_call(
        functools.partial(_nat_kernel, scale=scale, rows=rows),
        grid=(nbs, NAT_HEADS, nblk),
        in_specs=[
            pl.BlockSpec((tq, NAT_HD), lambda b, h, i: ((mp + b * ns) // tq + i, h)),
            pl.BlockSpec((ns, NAT_HD), lambda b, h, i: (mp // ns + b, NAT_HEADS + h)),
            pl.BlockSpec((ns, NAT_HD), lambda b, h, i: (mp // ns + b, 2 * NAT_HEADS + h)),
            pl.BlockSpec((past, NAT_HD), lambda b, h, i: (b, h)),
            pl.BlockSpec((past, NAT_HD), lambda b, h, i: (b, h)),
            pl.BlockSpec((1, 1, tq, NAT_K_ROWS * GRID_W), bias_idx),
        ],
        out_specs=pl.BlockSpec((tq, NAT_HD), lambda b, h, i: (b * nblk + i, h)),
        out_shape=jax.ShapeDtypeStruct((nbs * ns, NAT_HEADS * NAT_HD), BF16),
        compiler_params=_params(("parallel", "parallel", "arbitrary")),
        name="nat_latent",
    )(y, y, y, kc2d, vc2d, bias)


def _hgrn_tables():
    t_ = HG_TILE
    nl = int(math.log2(t_))
    a = np.zeros((2, (nl + 2) * t_, t_), np.float32)
    for lv in range(nl):
        w = 1 << lv
        for t in range(t_):
            mid = (t // (2 * w)) * 2 * w + w
            if t >= mid:
                a[0, lv * t_ + t, mid:t + 1] = 1
            else:
                a[0, lv * t_ + t, t + 1:mid] = 1
    for t in range(t_):
        a[0, nl * t_ + t, :t + 1] = 1
        a[0, (nl + 1) * t_ + t, t + 1:] = 1
    for blk in range(nl + 2):
        a[1, blk * t_:(blk + 1) * t_] = a[0, blk * t_:(blk + 1) * t_][::-1, ::-1]
    lev = np.full((2, t_, t_), -1, np.int32)
    for t in range(t_):
        for s in range(t_):
            if s == t:
                lev[:, t, s] = nl
            elif s < t:
                lev[0, t, s] = (t ^ s).bit_length() - 1
            else:
                lev[1, t, s] = (t ^ s).bit_length() - 1
    return a, lev


def _hgrn_kernel(q_ref, f0_ref, f1_ref, v_ref, gate_ref, lb_ref, go_ref, a_ref, lev_ref, s0_ref,
                 o_ref, st_ref, o_scr, *, n, has_state):
    t_ = HG_TILE
    nl = int(math.log2(t_))
    nt = n // t_
    for d in range(2):
        f_ref = f0_ref if d == 0 else f1_ref
        lb = lb_ref[d:d + 1, :]
        lev = lev_ref[d]

        def tile_step(ti, st, d=d, f_ref=f_ref, lb=lb, lev=lev):
            tile = ti if d == 0 else nt - 1 - ti
            rows = pl.ds(pl.multiple_of(tile * t_, t_), t_)
            qr = q_ref[rows, :]
            q = qr * jax.nn.sigmoid(qr)
            f = lb + (1.0 - lb) * jax.nn.sigmoid(f_ref[rows, :])
            k = 1.0 - f
            g = jnp.log(f)
            v = v_ref[rows, :].astype(BF16)
            g_hi = g.astype(BF16)
            g_lo = (g - g_hi.astype(F32)).astype(BF16)
            dd = jnp.dot(a_ref[d], jnp.concatenate([g_hi, g_lo], axis=1), preferred_element_type=F32)
            dsum = dd[:, :HG_K] + dd[:, HG_K:]
            att = jnp.where(lev == nl, _qkt(q.astype(BF16), k.astype(BF16)), 0.0)
            for lv in range(nl):
                e = jnp.exp(dsum[lv * t_:(lv + 1) * t_])
                att = jnp.where(lev == lv, _qkt((q * e).astype(BF16), (k * e).astype(BF16)), att)
            o = jnp.dot(att.astype(BF16), v, preferred_element_type=F32)
            q_in = (q * jnp.exp(dsum[nl * t_:(nl + 1) * t_])).astype(BF16)
            o = o + _qkt(q_in, st.astype(BF16))
            k_out = (k * jnp.exp(dsum[(nl + 1) * t_:(nl + 2) * t_])).astype(BF16)
            tot_row = nl * t_ + (t_ - 1 if d == 0 else 0)
            decay = jnp.exp(dsum[tot_row:tot_row + 1])
            st = st * decay + lax.dot_general(v, k_out, (((0,), (0,)), ((), ())), preferred_element_type=F32)
            if d == 0:
                o_scr[rows, :] = o
            else:
                o_scr[rows, :] += o
            return st

        st0 = s0_ref[0, d, 0].T if has_state else jnp.zeros((HG_V, HG_K), F32)
        st = lax.fori_loop(0, nt, tile_step, st0)
        st_ref[0, d, 0] = st.T
    o = o_scr[...]
    o = o * lax.rsqrt(jnp.mean(o * o, axis=-1, keepdims=True) + EPS) * go_ref[...]
    gate = gate_ref[...]
    o_ref[...] = (o * (gate * jax.nn.sigmoid(gate))).astype(o_ref.dtype)


def hgrn_scan(y, lb, g_o, s0, row_blk0, nb, n, tables):
    a, lev = tables
    hh = HG_HEADS
    has_state = s0 is not None
    if s0 is None:
        s0 = jnp.zeros((1, 2, 1, HG_K, HG_V), F32)
        s0_spec = pl.BlockSpec((1, 2, 1, HG_K, HG_V), lambda b, h: (0, 0, 0, 0, 0))
    else:
        s0_spec = pl.BlockSpec((1, 2, 1, HG_K, HG_V), lambda b, h: (b, 0, h, 0, 0))
    col = lambda c: pl.BlockSpec((n, LANES), lambda b, h: (row_blk0 + b, c * hh + h))
    return pl.pallas_call(
        functools.partial(_hgrn_kernel, n=n, has_state=has_state),
        grid=(nb, hh),
        in_specs=[
            col(0), col(1), col(2), col(3), col(4),
            pl.BlockSpec((2, HG_K), lambda b, h: (0, h)),
            pl.BlockSpec((1, HG_V), lambda b, h: (0, 0)),
            pl.BlockSpec(a.shape, lambda b, h: (0, 0, 0)),
            pl.BlockSpec(lev.shape, lambda b, h: (0, 0, 0)),
            s0_spec,
        ],
        out_specs=[
            pl.BlockSpec((n, HG_V), lambda b, h: (b, h)),
            pl.BlockSpec((1, 2, 1, HG_K, HG_V), lambda b, h: (b, 0, h, 0, 0)),
        ],
        out_shape=[
            jax.ShapeDtypeStruct((nb * n, hh * HG_V), BF16),
            jax.ShapeDtypeStruct((nb, 2, hh, HG_K, HG_V), F32),
        ],
        scratch_shapes=[pltpu.VMEM((n, HG_V), F32)],
        compiler_params=_params(("parallel", "arbitrary")),
        name="hgrn_scan",
    )(y, y, y, y, y, lb, g_o.reshape(1, HG_V), a, lev, s0)


def _pad_cols(x, width):
    return jnp.pad(x, [(0, 0)] * (x.ndim - 1) + [(0, width - x.shape[-1])])


def kernel(x_prompt, x_sample, cache_mla_ckv, cache_mla_kpe, state_hgrn, cache_nat_k, cache_nat_v, c, c_ctx, w_mod, b_mod, g_norm_mix, g_norm_ffn, w_mla_qa, g_mla_qa, w_mla_qb, g_mla_qn, w_mla_kva, g_mla_kva, w_mla_kvb, g_mla_kn, w_mla_o, w_hg_q, w_hg_f, hg_lower_bounds, w_hg_i, w_hg_g, g_hg_o, w_hg_o, w_nat_qkv, g_nat_q, g_nat_k, nat_rpb, w_nat_o, w_ff1, w_ff2):
    bp, seq, d = x_prompt.shape
    nbs, ns, _ = x_sample.shape
    past = cache_mla_ckv.shape[2]
    depth = w_mod.shape[0]
    mp, ms = bp * seq, nbs * ns
    tm = 512

    x = jnp.concatenate([x_prompt.reshape(mp, d), x_sample.reshape(ms, d)], axis=0)
    cvec = jnp.zeros((8, d), F32).at[0].set(c_ctx).at[1:1 + nbs].set(c)
    mods = adaln_all(cvec, w_mod, b_mod)

    lb_all = jnp.cumsum(jax.nn.softmax(hg_lower_bounds.astype(F32), axis=1), axis=1)
    lb_all = lb_all - lb_all[:, :1]
    rope_tabs = _rope_tables(tm, ns)
    hg_tabs = _hgrn_tables()
    hg_tabs = (jnp.asarray(hg_tabs[0], BF16), jnp.asarray(hg_tabs[1]))

    new_ckv, new_kpe, new_hg, new_nat_k, new_nat_v = [], [], [], [], []
    for i in range(depth):
        kind, j = i % 3, i // 3
        modrows = mods[i].reshape(8 * N_MOD, 1, d)
        if kind == 0:
            w1 = jnp.concatenate(
                [w_mla_kva[j], jnp.zeros((d, MLA_Q_LORA - MLA_KV_LORA - MLA_ROPE), F32), w_mla_qa[j]], axis=1
            ).astype(BF16)
            y1 = nm_matmul(x, g_norm_mix[i], modrows, 0, 1, w1, mp, ns, tm=tm, tn=MLA_Q_LORA)
            w_qb_pad = _pad_cols(w_mla_qb[j].reshape(MLA_Q_LORA, MLA_HEADS, MLA_QK), MLA_QK_PAD).reshape(
                MLA_Q_LORA, MLA_HEADS * MLA_QK_PAD).astype(BF16)
            g_qn_pad = _pad_cols(g_mla_qn[j].reshape(1, MLA_QK), MLA_QK_PAD)
            g_kn_pad = _pad_cols(g_mla_kn[j].reshape(1, MLA_QK), MLA_QK_PAD)
            w_kvb = w_mla_kvb[j].astype(BF16)
            q2d = mla_q(y1, g_mla_qa[j], w_qb_pad, g_qn_pad, rope_tabs, mp, ns, tm=tm)
            ckv, k2d, v2d = mla_kv(y1, 0, y1, MLA_KV_LORA // LANES, g_mla_kva[j], w_kvb, g_kn_pad, rope_tabs,
                                   mp, ns, tm=tm, normalize=True)
            ckv_c = cache_mla_ckv[:, j].reshape(nbs * past, MLA_KV_LORA)
            kpe_c = _pad_cols(cache_mla_kpe[:, j].reshape(nbs * past, MLA_ROPE), LANES)
            ident_tabs = tuple(t[:past] for t in _rope_tables(past, ns))
            _, kc2d, vc2d = mla_kv(ckv_c, 0, kpe_c, 0, g_mla_kva[j], w_kvb, g_kn_pad, ident_tabs,
                                   nbs * past, ns, tm=past, normalize=False)
            scale = MLA_QK ** -0.5
            o_p = attn_block(q2d, 0, k2d, 0, v2d, 0, bp, seq, heads=MLA_HEADS, dq=MLA_QK_PAD, dv=MLA_V, scale=scale)
            o_s = attn_latent(q2d, k2d, v2d, kc2d, vc2d, mp, ns, nbs, past, heads=MLA_HEADS, dq=MLA_QK_PAD,
                              dv=MLA_V, scale=scale, tq=512)
            o = jnp.concatenate([o_p, o_s], axis=0)
            w_o = w_mla_o[j]
            new_ckv.append(ckv[:mp].reshape(bp, seq, MLA_KV_LORA))
            new_kpe.append(y1[:mp, MLA_KV_LORA:MLA_KV_LORA + MLA_ROPE].reshape(bp, seq, MLA_ROPE))
        elif kind == 1:
            w_all = jnp.concatenate([w_hg_q[j], w_hg_f[j, 0], w_hg_f[j, 1], w_hg_i[j], w_hg_g[j]], axis=1).astype(BF16)
            y = nm_matmul(x, g_norm_mix[i], modrows, 0, 1, w_all, mp, ns, tm=tm, tn=1024)
            lb = lb_all[:, i]
            o_p, s_ctx = hgrn_scan(y, lb, g_hg_o[j], None, 0, bp, seq, hg_tabs)
            o_s, _ = hgrn_scan(y, lb, g_hg_o[j], state_hgrn[:, j], mp // ns, nbs, ns, hg_tabs)
            o = jnp.concatenate([o_p, o_s], axis=0)
            w_o = w_hg_o[j]
            new_hg.append(s_ctx)
        else:
            gains = jnp.concatenate([jnp.tile(g_nat_q[j], NAT_HEADS), jnp.tile(g_nat_k[j], NAT_HEADS),
                                     jnp.ones((d,), F32)]).reshape(1, 3 * d)
            y = nm_matmul(x, g_norm_mix[i], modrows, 0, 1, w_nat_qkv[j].astype(BF16), mp, ns, tm=tm, tn=1024,
                          gains=gains, n_norm_cols=2 * d, head=NAT_HD)
            scale = NAT_HD ** -0.5
            o_p = attn_block(y, 0, y, 1, y, 2, bp, seq, heads=NAT_HEADS, dq=NAT_HD, dv=NAT_HD, scale=scale)
            bias = _nat_bias(nat_rpb[j], ns // GRID_W)
            kc2d = cache_nat_k[:, j].reshape(nbs * past, d)
            vc2d = cache_nat_v[:, j].reshape(nbs * past, d)
            o_s = nat_latent(y, kc2d, vc2d, bias, mp, ns, nbs, past, scale=scale)
            o = jnp.concatenate([o_p, o_s], axis=0)
            w_o = w_nat_o[j]
            new_nat_k.append(y[:mp, d:2 * d].reshape(bp, seq, NAT_HEADS, NAT_HD))
            new_nat_v.append(y[:mp, 2 * d:].reshape(bp, seq, NAT_HEADS, NAT_HD))
        x = matmul_res(o, w_o.astype(BF16), x, modrows, 2, mp, ns, tm=tm)
        x = fused_mlp(x, g_norm_ffn[i], modrows, w_ff1[i].astype(BF16), w_ff2[i].astype(BF16), mp, ns, tm=tm, tf=1024)
    return (x[:mp].reshape(bp, seq, d), x[mp:].reshape(nbs, ns, d),
            jnp.stack(new_ckv, axis=1), jnp.stack(new_kpe, axis=1), jnp.stack(new_hg, axis=1),
            jnp.stack(new_nat_k, axis=1), jnp.stack(new_nat_v, axis=1))
```

```python
import functools
import math

import jax
import jax.numpy as jnp
import numpy as np
from jax import lax
from jax.experimental import pallas as pl
from jax.experimental.pallas import tpu as pltpu

F32 = jnp.float32
BF16 = jnp.bfloat16
EPS = 1e-6
ROPE_THETA = 10000.0
V7X_VMEM_LIMIT_BYTES = 56 * 1024 * 1024
LANES = 128
ROW_CHUNK = 256

N_MOD = 6
GRID_W = 64
MLA_HEADS, MLA_NOPE, MLA_ROPE, MLA_V = 16, 128, 64, 128
MLA_QK = MLA_NOPE + MLA_ROPE
MLA_QK_PAD = 256
MLA_Q_LORA, MLA_KV_LORA = 768, 512
HG_HEADS, HG_K, HG_V = 16, 128, 128
HG_TILE = 128
HG_MXU_LEVELS = 3
NAT_HEADS, NAT_HD = 16, 128
NAT_WIN_ROWS, NAT_WIN_COLS = 8, 16
NAT_Q_ROWS = 4
NAT_K_ROWS = 12
NEG_BIG = -1e30


def _params(sem):
    return pltpu.CompilerParams(dimension_semantics=sem, vmem_limit_bytes=V7X_VMEM_LIMIT_BYTES)


def _mod_row(i, tm, mp, ns):
    start = i * tm
    return jnp.where(start < mp, 0, 1 + (start - mp) // ns)


def _norm_modulate(x, g, shift, scale):
    y = x * lax.rsqrt(jnp.mean(x * x, axis=-1, keepdims=True) + EPS) * g
    return y * (1.0 + scale) + shift


def _adaln_kernel(c_ref, w_ref, b_ref, o_ref):
    c = c_ref[...]
    s = (c * jax.nn.sigmoid(c)).astype(BF16)
    o_ref[0] = jnp.dot(s, w_ref[0].astype(BF16), preferred_element_type=F32) + b_ref[0]


def adaln_all(cvec, w_mod, b_mod):
    depth, d, n = w_mod.shape
    rows = cvec.shape[0]
    tn = 1024
    return pl.pallas_call(
        _adaln_kernel,
        grid=(depth, n // tn),
        in_specs=[
            pl.BlockSpec((rows, d), lambda l, j: (0, 0)),
            pl.BlockSpec((1, d, tn), lambda l, j: (l, 0, j)),
            pl.BlockSpec((1, 1, tn), lambda l, j: (l, 0, j)),
        ],
        out_specs=pl.BlockSpec((1, rows, tn), lambda l, j: (l, 0, j)),
        out_shape=jax.ShapeDtypeStruct((depth, rows, n), F32),
        compiler_params=_params(("parallel", "parallel")),
        name="adaln",
    )(cvec, w_mod, b_mod.reshape(depth, 1, n))


def _nm_matmul_kernel(x_ref, g_ref, sh_ref, sc_ref, w_ref, gain_ref, o_ref, h_scr, *, n_norm_tiles, head):
    j = pl.program_id(1)
    tm = x_ref.shape[0]

    def emit(rows, h, normed):
        acc = jnp.dot(h, w_ref[...], preferred_element_type=F32)
        if not normed:
            o_ref[rows, :] = acc
            return
        for c in range(acc.shape[1] // head):
            cols = slice(c * head, (c + 1) * head)
            blk = acc[:, cols]
            r = lax.rsqrt(jnp.mean(blk * blk, axis=-1, keepdims=True) + EPS)
            o_ref[rows, cols] = blk * r * gain_ref[:, cols]

    @pl.when(j == 0)
    def _():
        for r0 in range(0, tm, ROW_CHUNK):
            rows = slice(r0, r0 + ROW_CHUNK)
            h = _norm_modulate(x_ref[rows, :], g_ref[...], sh_ref[0], sc_ref[0]).astype(BF16)
            h_scr[rows, :] = h
            emit(rows, h, n_norm_tiles > 0)

    if n_norm_tiles > 1:
        pl.when((j > 0) & (j < n_norm_tiles))(lambda: emit(slice(None), h_scr[...], True))
    pl.when(j >= max(n_norm_tiles, 1))(lambda: emit(slice(None), h_scr[...], False))


def nm_matmul(x, g, modrows, k_shift, k_scale, w, mp, ns, *, tm, tn, gains=None, n_norm_cols=0, head=LANES):
    m, d = x.shape
    n = w.shape[1]
    if gains is None:
        gains = jnp.ones((1, n), F32)
    row = functools.partial(_mod_row, tm=tm, mp=mp, ns=ns)
    return pl.pallas_call(
        functools.partial(_nm_matmul_kernel, n_norm_tiles=n_norm_cols // tn, head=head),
        grid=(m // tm, n // tn),
        in_specs=[
            pl.BlockSpec((tm, d), lambda i, j: (i, 0)),
            pl.BlockSpec((1, d), lambda i, j: (0, 0)),
            pl.BlockSpec((1, 1, d), lambda i, j: (row(i) * N_MOD + k_shift, 0, 0)),
            pl.BlockSpec((1, 1, d), lambda i, j: (row(i) * N_MOD + k_scale, 0, 0)),
            pl.BlockSpec((d, tn), lambda i, j: (0, j)),
            pl.BlockSpec((1, tn), lambda i, j: (0, j)),
        ],
        out_specs=pl.BlockSpec((tm, tn), lambda i, j: (i, j)),
        out_shape=jax.ShapeDtypeStruct((m, n), F32),
        scratch_shapes=[pltpu.VMEM((tm, d), BF16)],
        compiler_params=_params(("parallel", "arbitrary")),
        name="nm_matmul",
    )(x, g.reshape(1, d), modrows, modrows, w, gains)


def _matmul_res_kernel(ap_ref, as_ref, w_ref, x_ref, gt_ref, o_ref, *, n_prompt_tiles):
    def finish(a_ref):
        acc = jnp.dot(a_ref[...], w_ref[...], preferred_element_type=F32)
        o_ref[...] = x_ref[...] + gt_ref[0] * acc

    i = pl.program_id(0)
    pl.when(i < n_prompt_tiles)(lambda: finish(ap_ref))
    pl.when(i >= n_prompt_tiles)(lambda: finish(as_ref))


def matmul_res(a_p, a_s, w, x, modrows, k_gate, mp, ns, *, tm):
    kdim = a_p.shape[1]
    m = a_p.shape[0] + a_s.shape[0]
    n = w.shape[1]
    npt = mp // tm
    row = functools.partial(_mod_row, tm=tm, mp=mp, ns=ns)
    return pl.pallas_call(
        functools.partial(_matmul_res_kernel, n_prompt_tiles=npt),
        grid=(m // tm,),
        in_specs=[
            pl.BlockSpec((tm, kdim), lambda i: (jnp.minimum(i, npt - 1), 0)),
            pl.BlockSpec((tm, kdim), lambda i: (jnp.maximum(i - npt, 0), 0)),
            pl.BlockSpec((kdim, n), lambda i: (0, 0)),
            pl.BlockSpec((tm, n), lambda i: (i, 0)),
            pl.BlockSpec((1, 1, n), lambda i: (row(i) * N_MOD + k_gate, 0, 0)),
        ],
        out_specs=pl.BlockSpec((tm, n), lambda i: (i, 0)),
        out_shape=jax.ShapeDtypeStruct((m, n), F32),
        compiler_params=_params(("parallel",)),
        name="matmul_res",
    )(a_p, a_s, w, x, modrows)


def _mlp_kernel(x_ref, g_ref, sh_ref, sc_ref, gt_ref, w1_ref, w2_ref, o_ref, h_scr, acc_scr):
    f = pl.program_id(1)
    tm = x_ref.shape[0]

    def hidden(h):
        u = jnp.dot(h, w1_ref[...], preferred_element_type=F32)
        u = jnp.square(jnp.maximum(u, 0.0)).astype(BF16)
        return jnp.dot(u, w2_ref[...], preferred_element_type=F32)

    @pl.when(f == 0)
    def _():
        for r0 in range(0, tm, ROW_CHUNK):
            rows = slice(r0, r0 + ROW_CHUNK)
            h = _norm_modulate(x_ref[rows, :], g_ref[...], sh_ref[0], sc_ref[0]).astype(BF16)
            h_scr[rows, :] = h
            acc_scr[rows, :] = hidden(h)

    @pl.when(f > 0)
    def _():
        acc_scr[...] += hidden(h_scr[...])

    @pl.when(f == pl.num_programs(1) - 1)
    def _():
        o_ref[...] = x_ref[...] + gt_ref[0] * acc_scr[...]


def fused_mlp(x, g, modrows, w1, w2, mp, ns, *, tm, tf):
    m, d = x.shape
    ff = w1.shape[1]
    row = functools.partial(_mod_row, tm=tm, mp=mp, ns=ns)
    modspec = lambda k: pl.BlockSpec((1, 1, d), lambda i, f: (row(i) * N_MOD + k, 0, 0))
    return pl.pallas_call(
        _mlp_kernel,
        grid=(m // tm, ff // tf),
        in_specs=[
            pl.BlockSpec((tm, d), lambda i, f: (i, 0)),
            pl.BlockSpec((1, d), lambda i, f: (0, 0)),
            modspec(3), modspec(4), modspec(5),
            pl.BlockSpec((d, tf), lambda i, f: (0, f)),
            pl.BlockSpec((tf, d), lambda i, f: (f, 0)),
        ],
        out_specs=pl.BlockSpec((tm, d), lambda i, f: (i, 0)),
        out_shape=jax.ShapeDtypeStruct((m, d), F32),
        scratch_shapes=[pltpu.VMEM((tm, d), BF16), pltpu.VMEM((tm, d), F32)],
        compiler_params=_params(("parallel", "arbitrary")),
        name="fused_mlp",
    )(x, g.reshape(1, d), modrows, modrows, modrows, w1, w2)


def _rope_lanes(x, c_ref, s1_ref, s2_ref):
    return x * c_ref[...] + pltpu.roll(x, 96, 1) * s1_ref[...] + pltpu.roll(x, 32, 1) * s2_ref[...]


def _mla_q_kernel(qa_ref, gqa_ref, w_ref, gqn_ref, c_ref, s1_ref, s2_ref, o_ref):
    qa = qa_ref[...]
    qa = (qa * lax.rsqrt(jnp.mean(qa * qa, axis=-1, keepdims=True) + EPS) * gqa_ref[...]).astype(BF16)
    q = jnp.dot(qa, w_ref[...], preferred_element_type=F32)
    for h in range(MLA_HEADS):
        qh = q[:, h * MLA_QK_PAD:(h + 1) * MLA_QK_PAD]
        r = lax.rsqrt(jnp.sum(qh * qh, axis=-1, keepdims=True) / MLA_QK + EPS)
        qn = qh * r * gqn_ref[...]
        o_ref[:, h * MLA_QK_PAD:h * MLA_QK_PAD + LANES] = qn[:, :LANES].astype(BF16)
        o_ref[:, h * MLA_QK_PAD + LANES:(h + 1) * MLA_QK_PAD] = _rope_lanes(
            qn[:, LANES:], c_ref, s1_ref, s2_ref).astype(BF16)


def _rope_spec(tm, mp, ns):
    def idx(i):
        start = i * tm
        return (jnp.where(start < mp, 0, 1 + ((start - mp) % ns) // tm), 0)

    return pl.BlockSpec((tm, LANES), idx)


def mla_q(y1, g_qa, w_qb_pad, g_qn_pad, tables, mp, ns, *, tm):
    m = y1.shape[0]
    n = MLA_HEADS * MLA_QK_PAD
    rs = _rope_spec(tm, mp, ns)
    return pl.pallas_call(
        _mla_q_kernel,
        grid=(m // tm,),
        in_specs=[
            pl.BlockSpec((tm, MLA_Q_LORA), lambda i: (i, 1)),
            pl.BlockSpec((1, MLA_Q_LORA), lambda i: (0, 0)),
            pl.BlockSpec((MLA_Q_LORA, n), lambda i: (0, 0)),
            pl.BlockSpec((1, MLA_QK_PAD), lambda i: (0, 0)),
            rs, rs, rs,
        ],
        out_specs=pl.BlockSpec((tm, n), lambda i: (i, 0)),
        out_shape=jax.ShapeDtypeStruct((m, n), BF16),
        compiler_params=_params(("parallel",)),
        name="mla_q",
    )(y1, g_qa.reshape(1, MLA_Q_LORA), w_qb_pad, g_qn_pad, *tables)


def _mla_kv_kernel(ckv_ref, kpe_ref, gkva_ref, w_ref, gkn_ref, c_ref, s1_ref, s2_ref,
                   ckv_out_ref, k_ref, v_ref, *, normalize):
    ckv = ckv_ref[...]
    if normalize:
        ckv = ckv * lax.rsqrt(jnp.mean(ckv * ckv, axis=-1, keepdims=True) + EPS) * gkva_ref[...]
    ckv_out_ref[...] = ckv
    kv = jnp.dot(ckv.astype(BF16), w_ref[...], preferred_element_type=F32)
    kpe = kpe_ref[...]
    pe_ss = jnp.sum(kpe * kpe, axis=-1, keepdims=True)
    for h in range(MLA_HEADS):
        nope = kv[:, h * 2 * LANES:h * 2 * LANES + LANES]
        r = lax.rsqrt((jnp.sum(nope * nope, axis=-1, keepdims=True) + pe_ss) / MLA_QK + EPS)
        k_ref[:, h * MLA_QK_PAD:h * MLA_QK_PAD + LANES] = (nope * r * gkn_ref[:, :LANES]).astype(BF16)
        k_ref[:, h * MLA_QK_PAD + LANES:(h + 1) * MLA_QK_PAD] = _rope_lanes(
            kpe * r * gkn_ref[:, LANES:], c_ref, s1_ref, s2_ref).astype(BF16)
        v_ref[:, h * MLA_V:(h + 1) * MLA_V] = kv[:, h * 2 * LANES + LANES:(h + 1) * 2 * LANES].astype(BF16)


def mla_kv(ckv_src, ckv_col, kpe_src, kpe_col, g_kva, w_kvb, g_kn_pad, tables, mp, ns, *, tm, normalize):
    m = ckv_src.shape[0]
    rs = _rope_spec(tm, mp, ns)
    return pl.pallas_call(
        functools.partial(_mla_kv_kernel, normalize=normalize),
        grid=(m // tm,),
        in_specs=[
            pl.BlockSpec((tm, MLA_KV_LORA), lambda i: (i, ckv_col)),
            pl.BlockSpec((tm, LANES), lambda i: (i, kpe_col)),
            pl.BlockSpec((1, MLA_KV_LORA), lambda i: (0, 0)),
            pl.BlockSpec((MLA_KV_LORA, MLA_HEADS * 2 * LANES), lambda i: (0, 0)),
            pl.BlockSpec((1, MLA_QK_PAD), lambda i: (0, 0)),
            rs, rs, rs,
        ],
        out_specs=[
            pl.BlockSpec((tm, MLA_KV_LORA), lambda i: (i, 0)),
            pl.BlockSpec((tm, MLA_HEADS * MLA_QK_PAD), lambda i: (i, 0)),
            pl.BlockSpec((tm, MLA_HEADS * MLA_V), lambda i: (i, 0)),
        ],
        out_shape=[
            jax.ShapeDtypeStruct((m, MLA_KV_LORA), F32),
            jax.ShapeDtypeStruct((m, MLA_HEADS * MLA_QK_PAD), BF16),
            jax.ShapeDtypeStruct((m, MLA_HEADS * MLA_V), BF16),
        ],
        compiler_params=_params(("parallel",)),
        name="mla_kv",
    )(ckv_src, kpe_src, g_kva.reshape(1, MLA_KV_LORA), w_kvb, g_kn_pad, *tables)


def _rope_tables(tm, ns):
    n_freq = MLA_ROPE // 4
    inv = 1.0 / (ROPE_THETA ** (jnp.arange(n_freq, dtype=F32) / n_freq))
    t = jnp.arange(ns)
    row = (t // GRID_W).astype(F32)
    col = (t % GRID_W).astype(F32)
    ang = jnp.concatenate([row[:, None] * inv, col[:, None] * inv], axis=-1)
    cos, sin = jnp.cos(ang), jnp.sin(ang)
    half = MLA_ROPE // 2
    z = jnp.zeros((ns, LANES - 2 * half), F32)
    zh = jnp.zeros((ns, half), F32)
    c = jnp.concatenate([cos, cos, z], axis=-1)
    s1 = jnp.concatenate([-sin, zh, z], axis=-1)
    s2 = jnp.concatenate([zh, sin, z], axis=-1)
    ident = jnp.concatenate([jnp.ones((tm, 2 * half), F32), jnp.zeros((tm, LANES - 2 * half), F32)], axis=-1)
    zero = jnp.zeros((tm, LANES), F32)
    return (jnp.concatenate([ident, c], axis=0), jnp.concatenate([zero, s1], axis=0),
            jnp.concatenate([zero, s2], axis=0))


def _softmax_parts(scores):
    m = scores[0].max(axis=-1, keepdims=True)
    for s in scores[1:]:
        m = jnp.maximum(m, s.max(axis=-1, keepdims=True))
    ps = [jnp.exp(s - m) for s in scores]
    l = ps[0].sum(axis=-1, keepdims=True)
    for p in ps[1:]:
        l = l + p.sum(axis=-1, keepdims=True)
    return ps, l


def _qkt(q, k):
    return lax.dot_general(q, k, (((1,), (1,)), ((), ())), preferred_element_type=F32)


def _attn_block_kernel(q_ref, k_ref, v_ref, o_ref, *, heads, dq, dv):
    for h in range(heads):
        q = q_ref[:, h * dq:(h + 1) * dq].astype(BF16)
        k = k_ref[:, h * dq:(h + 1) * dq].astype(BF16)
        v = v_ref[:, h * dv:(h + 1) * dv].astype(BF16)
        (p,), l = _softmax_parts([_qkt(q, k)])
        o = jnp.dot(p.astype(BF16), v, preferred_element_type=F32) / l
        o_ref[:, h * dv:(h + 1) * dv] = o.astype(o_ref.dtype)


def attn_block(q_src, q_col, k_src, k_col, v_src, v_col, nb, seq, *, heads, dq, dv):
    return pl.pallas_call(
        functools.partial(_attn_block_kernel, heads=heads, dq=dq, dv=dv),
        grid=(nb,),
        in_specs=[
            pl.BlockSpec((seq, heads * dq), lambda b: (b, q_col)),
            pl.BlockSpec((seq, heads * dq), lambda b: (b, k_col)),
            pl.BlockSpec((seq, heads * dv), lambda b: (b, v_col)),
        ],
        out_specs=pl.BlockSpec((seq, heads * dv), lambda b: (b, 0)),
        out_shape=jax.ShapeDtypeStruct((nb * seq, heads * dv), BF16),
        compiler_params=_params(("parallel",)),
        name="attn_block",
    )(q_src, k_src, v_src)


def _attn_latent_kernel(q_ref, k_ref, v_ref, kc_ref, vc_ref, o_ref):
    q = q_ref[...]
    (p1, p2), l = _softmax_parts([_qkt(q, k_ref[...]), _qkt(q, kc_ref[...])])
    o = jnp.dot(p1.astype(BF16), v_ref[...], preferred_element_type=F32)
    o = o + jnp.dot(p2.astype(BF16), vc_ref[...], preferred_element_type=F32)
    o_ref[...] = (o / l).astype(o_ref.dtype)


def attn_latent(q2d, k2d, v2d, kc2d, vc2d, mp, ns, nbs, past, *, heads, dq, dv, tq):
    return pl.pallas_call(
        _attn_latent_kernel,
        grid=(nbs, heads, ns // tq),
        in_specs=[
            pl.BlockSpec((tq, dq), lambda b, h, i: ((mp + b * ns) // tq + i, h)),
            pl.BlockSpec((ns, dq), lambda b, h, i: (mp // ns + b, h)),
            pl.BlockSpec((ns, dv), lambda b, h, i: (mp // ns + b, h)),
            pl.BlockSpec((past, dq), lambda b, h, i: (b, h)),
            pl.BlockSpec((past, dv), lambda b, h, i: (b, h)),
        ],
        out_specs=pl.BlockSpec((tq, dv), lambda b, h, i: (b * (ns // tq) + i, h)),
        out_shape=jax.ShapeDtypeStruct((nbs * ns, heads * dv), BF16),
        compiler_params=_params(("parallel", "parallel", "arbitrary")),
        name="attn_latent",
    )(q2d, k2d, v2d, kc2d, vc2d)


def _nat_key_row0(i, rows):
    return jnp.clip(NAT_Q_ROWS * i - NAT_WIN_ROWS // 2, 0, rows - NAT_K_ROWS)


def _nat_kernel(q_ref, k_ref, v_ref, kc_ref, vc_ref, b_ref, o_ref, *, rows):
    i = pl.program_id(2)
    start = pl.multiple_of(_nat_key_row0(i, rows) * GRID_W, GRID_W)
    nk = NAT_K_ROWS * GRID_W
    q = q_ref[...].astype(BF16)
    kw = k_ref[pl.ds(start, nk), :].astype(BF16)
    vw = v_ref[pl.ds(start, nk), :].astype(BF16)
    s_win = _qkt(q, kw) + b_ref[0, 0]
    s_ctx = _qkt(q, kc_ref[...].astype(BF16))
    (p1, p2), l = _softmax_parts([s_win, s_ctx])
    o = jnp.dot(p1.astype(BF16), vw, preferred_element_type=F32)
    o = o + jnp.dot(p2.astype(BF16), vc_ref[...].astype(BF16), preferred_element_type=F32)
    o_ref[...] = (o / l).astype(o_ref.dtype)


def _nat_bias_variants(rows):
    return (0, 1, rows // NAT_Q_ROWS - 1)


def _nat_bias(rpb, rows):
    w = GRID_W
    n_dr, n_dc = 2 * NAT_WIN_ROWS - 1, 2 * NAT_WIN_COLS - 1
    variants = _nat_bias_variants(rows)
    sel_r = np.zeros((len(variants), NAT_Q_ROWS, NAT_K_ROWS, n_dr), np.float32)
    for vi, i in enumerate(variants):
        row0 = min(max(NAT_Q_ROWS * i - NAT_WIN_ROWS // 2, 0), rows - NAT_K_ROWS)
        for a in range(NAT_Q_ROWS):
            qr = NAT_Q_ROWS * i + a
            r0 = min(max(qr - NAT_WIN_ROWS // 2, 0), rows - NAT_WIN_ROWS)
            for kl in range(NAT_K_ROWS):
                kr = row0 + kl
                if r0 <= kr < r0 + NAT_WIN_ROWS:
                    sel_r[vi, a, kl, kr - qr + NAT_WIN_ROWS - 1] = 1
    sel_c = np.zeros((n_dc, w, w), np.float32)
    for qc in range(w):
        c0 = min(max(qc - NAT_WIN_COLS // 2, 0), w - NAT_WIN_COLS)
        for kc in range(c0, c0 + NAT_WIN_COLS):
            sel_c[kc - qc + NAT_WIN_COLS - 1, qc, kc] = 1
    valid = (sel_r.sum(-1) > 0)[:, :, None, :, None] & (sel_c.sum(0) > 0)[None, None, :, None, :]
    rows_sel = jnp.einsum("hrd,vakr->hvakd", rpb.astype(F32), sel_r, precision=lax.Precision.HIGHEST)
    bias = jnp.einsum("hvakd,dqc->hvaqkc", rows_sel, sel_c, precision=lax.Precision.HIGHEST)
    bias = jnp.where(valid[None], bias, NEG_BIG)
    return bias.reshape(rpb.shape[0], len(variants), NAT_Q_ROWS * w, NAT_K_ROWS * w)


def nat_latent(y, kc2d, vc2d, bias, mp, ns, nbs, past):
    rows = ns // GRID_W
    tq = NAT_Q_ROWS * GRID_W
    nblk = ns // tq
    last = nblk - 1

    def bias_idx(b, h, i):
        return (h, jnp.where(i == 0, 0, jnp.where(i == last, 2, 1)), 0, 0)

    return pl.pallas_call(
        functools.partial(_nat_kernel, rows=rows),
        grid=(nbs, NAT_HEADS, nblk),
        in_specs=[
            pl.BlockSpec((tq, NAT_HD), lambda b, h, i: ((mp + b * ns) // tq + i, h)),
            pl.BlockSpec((ns, NAT_HD), lambda b, h, i: (mp // ns + b, NAT_HEADS + h)),
            pl.BlockSpec((ns, NAT_HD), lambda b, h, i: (mp // ns + b, 2 * NAT_HEADS + h)),
            pl.BlockSpec((past, NAT_HD), lambda b, h, i: (b, h)),
            pl.BlockSpec((past, NAT_HD), lambda b, h, i: (b, h)),
            pl.BlockSpec((1, 1, tq, NAT_K_ROWS * GRID_W), bias_idx),
        ],
        out_specs=pl.BlockSpec((tq, NAT_HD), lambda b, h, i: (b * nblk + i, h)),
        out_shape=jax.ShapeDtypeStruct((nbs * ns, NAT_HEADS * NAT_HD), BF16),
        compiler_params=_params(("parallel", "parallel", "arbitrary")),
        name="nat_latent",
    )(y, y, y, kc2d, vc2d, bias)


def _hgrn_tables():
    t_ = HG_TILE
    nl = int(math.log2(t_))
    nm = HG_MXU_LEVELS
    a = np.zeros((2, (nm + 1) * t_, t_), np.float32)
    for lv in range(nm):
        w = 1 << lv
        for t in range(t_):
            mid = (t // (2 * w)) * 2 * w + w
            if t >= mid:
                a[0, lv * t_ + t, mid:t + 1] = 1
            else:
                a[0, lv * t_ + t, t + 1:mid] = 1
    for t in range(t_):
        a[0, nm * t_ + t, :t + 1] = 1
    for blk in range(nm + 1):
        a[1, blk * t_:(blk + 1) * t_] = a[0, blk * t_:(blk + 1) * t_][::-1, ::-1]
    lev = np.full((2, t_, t_), -1, np.int32)
    for t in range(t_):
        for s in range(t_):
            if s == t:
                lev[:, t, s] = nl
            elif s < t:
                lev[0, t, s] = (t ^ s).bit_length() - 1
            else:
                lev[1, t, s] = (t ^ s).bit_length() - 1
    return a, lev


def _hgrn_level_sums(cum, lv, d):
    w = 1 << lv
    parts = []
    for bs in range(0, HG_TILE, 2 * w):
        mid = bs + w
        brow = mid - 1 if d == 0 else mid
        b = cum[brow:brow + 1, :]
        lo, up = cum[bs:mid], cum[mid:bs + 2 * w]
        parts += [b - lo, up - b] if d == 0 else [lo - b, b - up]
    return jnp.concatenate(parts, axis=0)


def _hgrn_tile(d, rows, st, q_ref, f_ref, v_ref, lb, a_ref, lev):
    t_ = HG_TILE
    nl = int(math.log2(t_))
    nm = HG_MXU_LEVELS
    qr = q_ref[rows, :]
    q = qr * jax.nn.sigmoid(qr)
    f = lb + (1.0 - lb) * jax.nn.sigmoid(f_ref[rows, :])
    k = 1.0 - f
    g = jnp.log(f)
    v = v_ref[rows, :].astype(BF16)
    g_hi = g.astype(BF16)
    g_lo = (g - g_hi.astype(F32)).astype(BF16)
    dd = jnp.dot(a_ref[d], jnp.concatenate([g_hi, g_lo], axis=1), preferred_element_type=F32)
    dsum = dd[:, :HG_K] + dd[:, HG_K:]
    cum = dsum[nm * t_:(nm + 1) * t_]
    att = jnp.where(lev == nl, _qkt(q.astype(BF16), k.astype(BF16)), 0.0)
    for lv in range(nl):
        e = jnp.exp(dsum[lv * t_:(lv + 1) * t_] if lv < nm else _hgrn_level_sums(cum, lv, d))
        att = jnp.where(lev == lv, _qkt((q * e).astype(BF16), (k * e).astype(BF16)), att)
    o = jnp.dot(att.astype(BF16), v, preferred_element_type=F32)
    o = o + _qkt((q * jnp.exp(cum)).astype(BF16), st.astype(BF16))
    tot_row = t_ - 1 if d == 0 else 0
    total = cum[tot_row:tot_row + 1]
    k_out = (k * jnp.exp(total - cum)).astype(BF16)
    st = st * jnp.exp(total) + lax.dot_general(v, k_out, (((0,), (0,)), ((), ())), preferred_element_type=F32)
    return o, st


def _hgrn_kernel(q_ref, f0_ref, f1_ref, v_ref, gate_ref, lb_ref, go_ref, a_ref, lev_ref, s0_ref,
                 o_ref, st_ref, o_scr, *, n, has_state):
    t_ = HG_TILE
    nt = n // t_
    o_scr[...] = jnp.zeros_like(o_scr)
    lbs = (lb_ref[0:1, :], lb_ref[1:2, :])
    levs = (lev_ref[0], lev_ref[1])

    def tile_pair(ti, sts):
        new = []
        for d, f_ref in ((0, f0_ref), (1, f1_ref)):
            tile = ti if d == 0 else nt - 1 - ti
            rows = pl.ds(pl.multiple_of(tile * t_, t_), t_)
            o, st = _hgrn_tile(d, rows, sts[d], q_ref, f_ref, v_ref, lbs[d], a_ref, levs[d])
            o_scr[rows, :] += o
            new.append(st)
        return tuple(new)

    if has_state:
        st0 = (s0_ref[0, 0, 0].T, s0_ref[0, 1, 0].T)
    else:
        st0 = (jnp.zeros((HG_V, HG_K), F32), jnp.zeros((HG_V, HG_K), F32))
    sts = lax.fori_loop(0, nt, tile_pair, st0, unroll=2)
    st_ref[0, 0, 0] = sts[0].T
    st_ref[0, 1, 0] = sts[1].T
    o = o_scr[...]
    o = o * lax.rsqrt(jnp.mean(o * o, axis=-1, keepdims=True) + EPS) * go_ref[...]
    gate = gate_ref[...]
    o_ref[...] = (o * (gate * jax.nn.sigmoid(gate))).astype(o_ref.dtype)


def hgrn_scan(y, lb, g_o, s0, row_blk0, nb, n, tables):
    a, lev = tables
    hh = HG_HEADS
    has_state = s0 is not None
    if s0 is None:
        s0 = jnp.zeros((1, 2, 1, HG_K, HG_V), F32)
        s0_spec = pl.BlockSpec((1, 2, 1, HG_K, HG_V), lambda b, h: (0, 0, 0, 0, 0))
    else:
        s0_spec = pl.BlockSpec((1, 2, 1, HG_K, HG_V), lambda b, h: (b, 0, h, 0, 0))
    col = lambda c: pl.BlockSpec((n, LANES), lambda b, h: (row_blk0 + b, c * hh + h))
    return pl.pallas_call(
        functools.partial(_hgrn_kernel, n=n, has_state=has_state),
        grid=(nb, hh),
        in_specs=[
            col(0), col(1), col(2), col(3), col(4),
            pl.BlockSpec((2, HG_K), lambda b, h: (0, h)),
            pl.BlockSpec((1, HG_V), lambda b, h: (0, 0)),
            pl.BlockSpec(a.shape, lambda b, h: (0, 0, 0)),
            pl.BlockSpec(lev.shape, lambda b, h: (0, 0, 0)),
            s0_spec,
        ],
        out_specs=[
            pl.BlockSpec((n, HG_V), lambda b, h: (b, h)),
            pl.BlockSpec((1, 2, 1, HG_K, HG_V), lambda b, h: (b, 0, h, 0, 0)),
        ],
        out_shape=[
            jax.ShapeDtypeStruct((nb * n, hh * HG_V), BF16),
            jax.ShapeDtypeStruct((nb, 2, hh, HG_K, HG_V), F32),
        ],
        scratch_shapes=[pltpu.VMEM((n, HG_V), F32)],
        compiler_params=_params(("parallel", "arbitrary")),
        name="hgrn_scan",
    )(y, y, y, y, y, lb, g_o.reshape(1, HG_V), a, lev, s0)


def _pad_cols(x, width):
    return jnp.pad(x, [(0, 0)] * (x.ndim - 1) + [(0, width - x.shape[-1])])


def kernel(x_prompt, x_sample, cache_mla_ckv, cache_mla_kpe, state_hgrn, cache_nat_k, cache_nat_v, c, c_ctx, w_mod, b_mod, g_norm_mix, g_norm_ffn, w_mla_qa, g_mla_qa, w_mla_qb, g_mla_qn, w_mla_kva, g_mla_kva, w_mla_kvb, g_mla_kn, w_mla_o, w_hg_q, w_hg_f, hg_lower_bounds, w_hg_i, w_hg_g, g_hg_o, w_hg_o, w_nat_qkv, g_nat_q, g_nat_k, nat_rpb, w_nat_o, w_ff1, w_ff2):
    bp, seq, d = x_prompt.shape
    nbs, ns, _ = x_sample.shape
    past = cache_mla_ckv.shape[2]
    depth = w_mod.shape[0]
    mp, ms = bp * seq, nbs * ns
    tm = 512
    tm_proj = 1024

    x = jnp.concatenate([x_prompt.reshape(mp, d), x_sample.reshape(ms, d)], axis=0)
    cvec = jnp.zeros((8, d), F32).at[0].set(c_ctx).at[1:1 + nbs].set(c)
    mods = adaln_all(cvec, w_mod, b_mod)

    lb_all = jnp.cumsum(jax.nn.softmax(hg_lower_bounds.astype(F32), axis=1), axis=1)
    lb_all = lb_all - lb_all[:, :1]
    rope_tabs = _rope_tables(tm, ns)
    hg_tabs = _hgrn_tables()
    hg_tabs = (jnp.asarray(hg_tabs[0], BF16), jnp.asarray(hg_tabs[1]))

    new_ckv, new_kpe, new_hg, new_nat_k, new_nat_v = [], [], [], [], []
    for i in range(depth):
        kind, j = i % 3, i // 3
        modrows = mods[i].reshape(8 * N_MOD, 1, d)
        if kind == 0:
            w1 = jnp.concatenate(
                [w_mla_kva[j], jnp.zeros((d, MLA_Q_LORA - MLA_KV_LORA - MLA_ROPE), F32), w_mla_qa[j]], axis=1
            ).astype(BF16)
            y1 = nm_matmul(x, g_norm_mix[i], modrows, 0, 1, w1, mp, ns, tm=tm_proj, tn=MLA_Q_LORA)
            w_qb_pad = _pad_cols(w_mla_qb[j].reshape(MLA_Q_LORA, MLA_HEADS, MLA_QK), MLA_QK_PAD).reshape(
                MLA_Q_LORA, MLA_HEADS * MLA_QK_PAD).astype(BF16)
            g_qn_pad = _pad_cols(g_mla_qn[j].reshape(1, MLA_QK), MLA_QK_PAD) * MLA_QK ** -0.5
            g_kn_pad = _pad_cols(g_mla_kn[j].reshape(1, MLA_QK), MLA_QK_PAD)
            w_kvb = w_mla_kvb[j].astype(BF16)
            q2d = mla_q(y1, g_mla_qa[j], w_qb_pad, g_qn_pad, rope_tabs, mp, ns, tm=tm)
            ckv, k2d, v2d = mla_kv(y1, 0, y1, MLA_KV_LORA // LANES, g_mla_kva[j], w_kvb, g_kn_pad, rope_tabs,
                                   mp, ns, tm=tm, normalize=True)
            ckv_c = cache_mla_ckv[:, j].reshape(nbs * past, MLA_KV_LORA)
            kpe_c = _pad_cols(cache_mla_kpe[:, j].reshape(nbs * past, MLA_ROPE), LANES)
            ident_tabs = tuple(t[:past] for t in _rope_tables(past, ns))
            _, kc2d, vc2d = mla_kv(ckv_c, 0, kpe_c, 0, g_mla_kva[j], w_kvb, g_kn_pad, ident_tabs,
                                   nbs * past, ns, tm=past, normalize=False)
            o_p = attn_block(q2d, 0, k2d, 0, v2d, 0, bp, seq, heads=MLA_HEADS, dq=MLA_QK_PAD, dv=MLA_V)
            o_s = attn_latent(q2d, k2d, v2d, kc2d, vc2d, mp, ns, nbs, past, heads=MLA_HEADS, dq=MLA_QK_PAD,
                              dv=MLA_V, tq=512)
            w_o = w_mla_o[j]
            new_ckv.append(ckv[:mp].reshape(bp, seq, MLA_KV_LORA))
            new_kpe.append(y1[:mp, MLA_KV_LORA:MLA_KV_LORA + MLA_ROPE].reshape(bp, seq, MLA_ROPE))
        elif kind == 1:
            w_all = jnp.concatenate([w_hg_q[j], w_hg_f[j, 0], w_hg_f[j, 1], w_hg_i[j], w_hg_g[j]], axis=1).astype(BF16)
            y = nm_matmul(x, g_norm_mix[i], modrows, 0, 1, w_all, mp, ns, tm=tm_proj, tn=1024)
            lb = lb_all[:, i]
            o_p, s_ctx = hgrn_scan(y, lb, g_hg_o[j], None, 0, bp, seq, hg_tabs)
            o_s, _ = hgrn_scan(y, lb, g_hg_o[j], state_hgrn[:, j], mp // ns, nbs, ns, hg_tabs)
            w_o = w_hg_o[j]
            new_hg.append(s_ctx)
        else:
            gains = jnp.concatenate([jnp.tile(g_nat_q[j] * NAT_HD ** -0.5, NAT_HEADS), jnp.tile(g_nat_k[j], NAT_HEADS),
                                     jnp.ones((d,), F32)]).reshape(1, 3 * d)
            y = nm_matmul(x, g_norm_mix[i], modrows, 0, 1, w_nat_qkv[j].astype(BF16), mp, ns, tm=tm_proj, tn=1024,
                          gains=gains, n_norm_cols=2 * d, head=NAT_HD)
            o_p = attn_block(y, 0, y, 1, y, 2, bp, seq, heads=NAT_HEADS, dq=NAT_HD, dv=NAT_HD)
            bias = _nat_bias(nat_rpb[j], ns // GRID_W)
            kc2d = cache_nat_k[:, j].reshape(nbs * past, d)
            vc2d = cache_nat_v[:, j].reshape(nbs * past, d)
            o_s = nat_latent(y, kc2d, vc2d, bias, mp, ns, nbs, past)
            w_o = w_nat_o[j]
            new_nat_k.append(y[:mp, d:2 * d].reshape(bp, seq, NAT_HEADS, NAT_HD))
            new_nat_v.append(y[:mp, 2 * d:].reshape(bp, seq, NAT_HEADS, NAT_HD))
        x = matmul_res(o_p, o_s, w_o.astype(BF16), x, modrows, 2, mp, ns, tm=tm)
        x = fused_mlp(x, g_norm_ffn[i], modrows, w_ff1[i].astype(BF16), w_ff2[i].astype(BF16), mp, ns, tm=tm, tf=1024)
    return (x[:mp].reshape(bp, seq, d), x[mp:].reshape(nbs, ns, d),
            jnp.stack(new_ckv, axis=1), jnp.stack(new_kpe, axis=1), jnp.stack(new_hg, axis=1),
            jnp.stack(new_nat_k, axis=1), jnp.stack(new_nat_v, axis=1))
```

```python
import functools
import math

import jax
import jax.numpy as jnp
import numpy as np
from jax import lax
from jax.experimental import pallas as pl
from jax.experimental.pallas import tpu as pltpu

F32 = jnp.float32
BF16 = jnp.bfloat16
EPS = 1e-6
ROPE_THETA = 10000.0
V7X_VMEM_LIMIT_BYTES = 56 * 1024 * 1024
LANES = 128
ROW_CHUNK = 256
ATTN_ROW_CHUNK = 256

N_MOD = 6
GRID_W = 64
MLA_HEADS, MLA_NOPE, MLA_ROPE, MLA_V = 16, 128, 64, 128
MLA_QK = MLA_NOPE + MLA_ROPE
MLA_QK_PAD = 256
MLA_Q_LORA, MLA_KV_LORA = 768, 512
HG_HEADS, HG_K, HG_V = 16, 128, 128
HG_TILE = 128
HG_HEADS_PER_STEP = 2
HG_UNROLL = 1
HG_MXU_LEVELS = 3
NAT_HEADS, NAT_HD = 16, 128
NAT_WIN_ROWS, NAT_WIN_COLS = 8, 16
NAT_Q_ROWS = 4
NAT_K_ROWS = 12
NEG_BIG = -1e30
LOG2E = math.log2(math.e)


def _params(sem):
    return pltpu.CompilerParams(dimension_semantics=sem, vmem_limit_bytes=V7X_VMEM_LIMIT_BYTES)


def _mod_row(i, tm, mp, ns):
    start = i * tm
    return jnp.where(start < mp, 0, 1 + (start - mp) // ns)


def _sigmoid(x):
    return 0.5 * jnp.tanh(0.5 * x) + 0.5


def _norm_modulate(x, g, shift, scale):
    y = x * lax.rsqrt(jnp.mean(x * x, axis=-1, keepdims=True) + EPS) * g
    return y * (1.0 + scale) + shift


def _adaln_kernel(c_ref, w_ref, b_ref, o_ref):
    c = c_ref[...]
    s = (c * jax.nn.sigmoid(c)).astype(BF16)
    o_ref[0] = jnp.dot(s, w_ref[0].astype(BF16), preferred_element_type=F32) + b_ref[0]


def adaln_all(cvec, w_mod, b_mod):
    depth, d, n = w_mod.shape
    rows = cvec.shape[0]
    tn = 1024
    return pl.pallas_call(
        _adaln_kernel,
        grid=(depth, n // tn),
        in_specs=[
            pl.BlockSpec((rows, d), lambda l, j: (0, 0)),
            pl.BlockSpec((1, d, tn), lambda l, j: (l, 0, j)),
            pl.BlockSpec((1, 1, tn), lambda l, j: (l, 0, j)),
        ],
        out_specs=pl.BlockSpec((1, rows, tn), lambda l, j: (l, 0, j)),
        out_shape=jax.ShapeDtypeStruct((depth, rows, n), F32),
        compiler_params=_params(("parallel", "parallel")),
        name="adaln",
    )(cvec, w_mod, b_mod.reshape(depth, 1, n))


def _nm_matmul_kernel(x_ref, g_ref, sh_ref, sc_ref, w_ref, gain_ref, o_ref, h_scr, *, n_norm_tiles, head):
    j = pl.program_id(1)
    tm = x_ref.shape[0]

    def emit(rows, h, normed):
        acc = jnp.dot(h, w_ref[...], preferred_element_type=F32)
        if not normed:
            o_ref[rows, :] = acc
            return
        for c in range(acc.shape[1] // head):
            cols = slice(c * head, (c + 1) * head)
            blk = acc[:, cols]
            r = lax.rsqrt(jnp.mean(blk * blk, axis=-1, keepdims=True) + EPS)
            o_ref[rows, cols] = blk * r * gain_ref[:, cols]

    @pl.when(j == 0)
    def _():
        for r0 in range(0, tm, ROW_CHUNK):
            rows = slice(r0, r0 + ROW_CHUNK)
            h = _norm_modulate(x_ref[rows, :], g_ref[...], sh_ref[0], sc_ref[0]).astype(BF16)
            h_scr[rows, :] = h
            emit(rows, h, n_norm_tiles > 0)

    if n_norm_tiles > 1:
        pl.when((j > 0) & (j < n_norm_tiles))(lambda: emit(slice(None), h_scr[...], True))
    pl.when(j >= max(n_norm_tiles, 1))(lambda: emit(slice(None), h_scr[...], False))


def nm_matmul(x, g, modrows, k_shift, k_scale, w, mp, ns, *, tm, tn, gains=None, n_norm_cols=0, head=LANES):
    m, d = x.shape
    n = w.shape[1]
    if gains is None:
        gains = jnp.ones((1, n), F32)
    row = functools.partial(_mod_row, tm=tm, mp=mp, ns=ns)
    return pl.pallas_call(
        functools.partial(_nm_matmul_kernel, n_norm_tiles=n_norm_cols // tn, head=head),
        grid=(m // tm, n // tn),
        in_specs=[
            pl.BlockSpec((tm, d), lambda i, j: (i, 0)),
            pl.BlockSpec((1, d), lambda i, j: (0, 0)),
            pl.BlockSpec((1, 1, d), lambda i, j: (row(i) * N_MOD + k_shift, 0, 0)),
            pl.BlockSpec((1, 1, d), lambda i, j: (row(i) * N_MOD + k_scale, 0, 0)),
            pl.BlockSpec((d, tn), lambda i, j: (0, j)),
            pl.BlockSpec((1, tn), lambda i, j: (0, j)),
        ],
        out_specs=pl.BlockSpec((tm, tn), lambda i, j: (i, j)),
        out_shape=jax.ShapeDtypeStruct((m, n), F32),
        scratch_shapes=[pltpu.VMEM((tm, d), BF16)],
        compiler_params=_params(("parallel", "arbitrary")),
        name="nm_matmul",
    )(x, g.reshape(1, d), modrows, modrows, w, gains)


def _matmul_res_kernel(ap_ref, as_ref, w_ref, x_ref, gt_ref, o_ref, *, n_prompt_tiles):
    def finish(a_ref):
        acc = jnp.dot(a_ref[...], w_ref[...], preferred_element_type=F32)
        o_ref[...] = x_ref[...] + gt_ref[0] * acc

    i = pl.program_id(0)
    pl.when(i < n_prompt_tiles)(lambda: finish(ap_ref))
    pl.when(i >= n_prompt_tiles)(lambda: finish(as_ref))


def matmul_res(a_p, a_s, w, x, modrows, k_gate, mp, ns, *, tm):
    kdim = a_p.shape[1]
    m = a_p.shape[0] + a_s.shape[0]
    n = w.shape[1]
    npt = mp // tm
    row = functools.partial(_mod_row, tm=tm, mp=mp, ns=ns)
    return pl.pallas_call(
        functools.partial(_matmul_res_kernel, n_prompt_tiles=npt),
        grid=(m // tm,),
        in_specs=[
            pl.BlockSpec((tm, kdim), lambda i: (jnp.minimum(i, npt - 1), 0)),
            pl.BlockSpec((tm, kdim), lambda i: (jnp.maximum(i - npt, 0), 0)),
            pl.BlockSpec((kdim, n), lambda i: (0, 0)),
            pl.BlockSpec((tm, n), lambda i: (i, 0)),
            pl.BlockSpec((1, 1, n), lambda i: (row(i) * N_MOD + k_gate, 0, 0)),
        ],
        out_specs=pl.BlockSpec((tm, n), lambda i: (i, 0)),
        out_shape=jax.ShapeDtypeStruct((m, n), F32),
        compiler_params=_params(("parallel",)),
        name="matmul_res",
    )(a_p, a_s, w, x, modrows)


def _mlp_kernel(x_ref, g_ref, sh_ref, sc_ref, gt_ref, w1_ref, w2_ref, o_ref, h_scr, acc_scr):
    f = pl.program_id(1)
    tm = x_ref.shape[0]

    def hidden(h):
        u = jnp.dot(h, w1_ref[...], preferred_element_type=F32)
        u = jnp.square(jnp.maximum(u, 0.0)).astype(BF16)
        return jnp.dot(u, w2_ref[...], preferred_element_type=F32)

    @pl.when(f == 0)
    def _():
        for r0 in range(0, tm, ROW_CHUNK):
            rows = slice(r0, r0 + ROW_CHUNK)
            h = _norm_modulate(x_ref[rows, :], g_ref[...], sh_ref[0], sc_ref[0]).astype(BF16)
            h_scr[rows, :] = h
            acc_scr[rows, :] = hidden(h)

    @pl.when(f > 0)
    def _():
        acc_scr[...] += hidden(h_scr[...])

    @pl.when(f == pl.num_programs(1) - 1)
    def _():
        o_ref[...] = x_ref[...] + gt_ref[0] * acc_scr[...]


def fused_mlp(x, g, modrows, w1, w2, layer, mp, ns, *, tm, tf):
    m, d = x.shape
    ff = w1.shape[2]
    row = functools.partial(_mod_row, tm=tm, mp=mp, ns=ns)
    modspec = lambda k: pl.BlockSpec((1, 1, d), lambda i, f: (row(i) * N_MOD + k, 0, 0))
    return pl.pallas_call(
        _mlp_kernel,
        grid=(m // tm, ff // tf),
        in_specs=[
            pl.BlockSpec((tm, d), lambda i, f: (i, 0)),
            pl.BlockSpec((1, d), lambda i, f: (0, 0)),
            modspec(3), modspec(4), modspec(5),
            pl.BlockSpec((None, d, tf), lambda i, f: (layer, 0, f)),
            pl.BlockSpec((None, tf, d), lambda i, f: (layer, f, 0)),
        ],
        out_specs=pl.BlockSpec((tm, d), lambda i, f: (i, 0)),
        out_shape=jax.ShapeDtypeStruct((m, d), F32),
        scratch_shapes=[pltpu.VMEM((tm, d), BF16), pltpu.VMEM((tm, d), F32)],
        compiler_params=_params(("parallel", "arbitrary")),
        name="fused_mlp",
    )(x, g.reshape(1, d), modrows, modrows, modrows, w1, w2)


def _rope_lanes(x, c_ref, s_ref):
    return x * c_ref[...] + pltpu.roll(x, LANES // 2, 1) * s_ref[...]


def _swap_tail(x):
    half = MLA_ROPE // 2
    pe = x[..., -MLA_ROPE:]
    return jnp.concatenate([x, pe[..., half:], pe[..., :half]], axis=-1)


def _mla_q_kernel(qa_ref, gqa_ref, w_ref, gqn_ref, c_ref, s_ref, o_ref):
    qa = qa_ref[...]
    qa = (qa * lax.rsqrt(jnp.mean(qa * qa, axis=-1, keepdims=True) + EPS) * gqa_ref[...]).astype(BF16)
    q = jnp.dot(qa, w_ref[...], preferred_element_type=F32)
    for h in range(MLA_HEADS):
        nope = q[:, h * MLA_QK_PAD:h * MLA_QK_PAD + LANES]
        pe = q[:, h * MLA_QK_PAD + LANES:(h + 1) * MLA_QK_PAD]
        ss = jnp.sum(nope * nope + 0.5 * (pe * pe), axis=-1, keepdims=True)
        r = lax.rsqrt(ss / MLA_QK + EPS)
        o_ref[:, h * MLA_QK_PAD:h * MLA_QK_PAD + LANES] = (nope * r * gqn_ref[:, :LANES]).astype(BF16)
        o_ref[:, h * MLA_QK_PAD + LANES:(h + 1) * MLA_QK_PAD] = _rope_lanes(
            pe * r * gqn_ref[:, LANES:], c_ref, s_ref).astype(BF16)


def _rope_spec(tm, mp, ns):
    def idx(i):
        start = i * tm
        return (jnp.where(start < mp, 0, 1 + ((start - mp) % ns) // tm), 0)

    return pl.BlockSpec((tm, LANES), idx)


def mla_q(y1, g_qa, w_qb_pad, g_qn_pad, tables, mp, ns, *, tm):
    m = y1.shape[0]
    n = MLA_HEADS * MLA_QK_PAD
    rs = _rope_spec(tm, mp, ns)
    return pl.pallas_call(
        _mla_q_kernel,
        grid=(m // tm,),
        in_specs=[
            pl.BlockSpec((tm, MLA_Q_LORA), lambda i: (i, 1)),
            pl.BlockSpec((1, MLA_Q_LORA), lambda i: (0, 0)),
            pl.BlockSpec((MLA_Q_LORA, n), lambda i: (0, 0)),
            pl.BlockSpec((1, MLA_QK_PAD), lambda i: (0, 0)),
            rs, rs,
        ],
        out_specs=pl.BlockSpec((tm, n), lambda i: (i, 0)),
        out_shape=jax.ShapeDtypeStruct((m, n), BF16),
        compiler_params=_params(("parallel",)),
        name="mla_q",
    )(y1, g_qa.reshape(1, MLA_Q_LORA), w_qb_pad, g_qn_pad, *tables)


def _mla_kv_kernel(ckv_ref, kpe_ref, gkva_ref, w_ref, gkn_ref, c_ref, s_ref,
                   ckv_out_ref, k_ref, v_ref, *, normalize):
    ckv = ckv_ref[...]
    if normalize:
        ckv = ckv * lax.rsqrt(jnp.mean(ckv * ckv, axis=-1, keepdims=True) + EPS) * gkva_ref[...]
    ckv_out_ref[...] = ckv
    kv = jnp.dot(ckv.astype(BF16), w_ref[...], preferred_element_type=F32)
    kpe = kpe_ref[...]
    pe_ss = 0.5 * jnp.sum(kpe * kpe, axis=-1, keepdims=True)
    pe_rot = _rope_lanes(kpe * gkn_ref[:, LANES:], c_ref, s_ref)
    for h in range(MLA_HEADS):
        nope = kv[:, h * 2 * LANES:h * 2 * LANES + LANES]
        r = lax.rsqrt((jnp.sum(nope * nope, axis=-1, keepdims=True) + pe_ss) / MLA_QK + EPS)
        k_ref[:, h * MLA_QK_PAD:h * MLA_QK_PAD + LANES] = (nope * r * gkn_ref[:, :LANES]).astype(BF16)
        k_ref[:, h * MLA_QK_PAD + LANES:(h + 1) * MLA_QK_PAD] = (pe_rot * r).astype(BF16)
        v_ref[:, h * MLA_V:(h + 1) * MLA_V] = kv[:, h * 2 * LANES + LANES:(h + 1) * 2 * LANES].astype(BF16)


def mla_kv(ckv_src, ckv_col, kpe_src, kpe_col, g_kva, w_kvb, g_kn_pad, tables, mp, ns, *, tm, normalize):
    m = ckv_src.shape[0]
    rs = _rope_spec(tm, mp, ns)
    return pl.pallas_call(
        functools.partial(_mla_kv_kernel, normalize=normalize),
        grid=(m // tm,),
        in_specs=[
            pl.BlockSpec((tm, MLA_KV_LORA), lambda i: (i, ckv_col)),
            pl.BlockSpec((tm, LANES), lambda i: (i, kpe_col)),
            pl.BlockSpec((1, MLA_KV_LORA), lambda i: (0, 0)),
            pl.BlockSpec((MLA_KV_LORA, MLA_HEADS * 2 * LANES), lambda i: (0, 0)),
            pl.BlockSpec((1, MLA_QK_PAD), lambda i: (0, 0)),
            rs, rs,
        ],
        out_specs=[
            pl.BlockSpec((tm, MLA_KV_LORA), lambda i: (i, 0)),
            pl.BlockSpec((tm, MLA_HEADS * MLA_QK_PAD), lambda i: (i, 0)),
            pl.BlockSpec((tm, MLA_HEADS * MLA_V), lambda i: (i, 0)),
        ],
        out_shape=[
            jax.ShapeDtypeStruct((m, MLA_KV_LORA), F32),
            jax.ShapeDtypeStruct((m, MLA_HEADS * MLA_QK_PAD), BF16),
            jax.ShapeDtypeStruct((m, MLA_HEADS * MLA_V), BF16),
        ],
        compiler_params=_params(("parallel",)),
        name="mla_kv",
    )(ckv_src, kpe_src, g_kva.reshape(1, MLA_KV_LORA), w_kvb, g_kn_pad, *tables)


def _rope_tables(tm, ns):
    n_freq = MLA_ROPE // 4
    inv = 1.0 / (ROPE_THETA ** (jnp.arange(n_freq, dtype=F32) / n_freq))
    t = jnp.arange(ns)
    row = (t // GRID_W).astype(F32)
    col = (t % GRID_W).astype(F32)
    ang = jnp.concatenate([row[:, None] * inv, col[:, None] * inv], axis=-1)
    cos, sin = jnp.cos(ang), jnp.sin(ang)
    z = jnp.zeros((ns, LANES - MLA_ROPE), F32)
    c = jnp.concatenate([cos, cos, z], axis=-1)
    s = jnp.concatenate([-sin, sin, z], axis=-1)
    ident = jnp.concatenate([jnp.ones((tm, MLA_ROPE), F32), jnp.zeros((tm, LANES - MLA_ROPE), F32)], axis=-1)
    return jnp.concatenate([ident, c], axis=0), jnp.concatenate([jnp.zeros((tm, LANES), F32), s], axis=0)


def _softmax_parts(scores):
    m = scores[0].max(axis=-1, keepdims=True)
    for s in scores[1:]:
        m = jnp.maximum(m, s.max(axis=-1, keepdims=True))
    ps = [jnp.exp2(s - m) for s in scores]
    l = ps[0].sum(axis=-1, keepdims=True)
    for p in ps[1:]:
        l = l + p.sum(axis=-1, keepdims=True)
    return ps, l


def _qkt(q, k):
    return lax.dot_general(q, k, (((1,), (1,)), ((), ())), preferred_element_type=F32)


def _attn_block_kernel(q_ref, k_ref, v_ref, o_ref, *, heads, dq, dv):
    for h in range(heads):
        q = q_ref[:, h * dq:(h + 1) * dq].astype(BF16)
        k = k_ref[:, h * dq:(h + 1) * dq].astype(BF16)
        v = v_ref[:, h * dv:(h + 1) * dv].astype(BF16)
        (p,), l = _softmax_parts([_qkt(q, k)])
        o = jnp.dot(p.astype(BF16), v, preferred_element_type=F32) / l
        o_ref[:, h * dv:(h + 1) * dv] = o.astype(o_ref.dtype)


def attn_block(q_src, q_col, k_src, k_col, v_src, v_col, nb, seq, *, heads, dq, dv):
    return pl.pallas_call(
        functools.partial(_attn_block_kernel, heads=heads, dq=dq, dv=dv),
        grid=(nb,),
        in_specs=[
            pl.BlockSpec((seq, heads * dq), lambda b: (b, q_col)),
            pl.BlockSpec((seq, heads * dq), lambda b: (b, k_col)),
            pl.BlockSpec((seq, heads * dv), lambda b: (b, v_col)),
        ],
        out_specs=pl.BlockSpec((seq, heads * dv), lambda b: (b, 0)),
        out_shape=jax.ShapeDtypeStruct((nb * seq, heads * dv), BF16),
        compiler_params=_params(("parallel",)),
        name="attn_block",
    )(q_src, k_src, v_src)


def _attn_latent_kernel(q_ref, k_ref, v_ref, kc_ref, vc_ref, o_ref):
    for r0 in range(0, q_ref.shape[0], ATTN_ROW_CHUNK):
        rows = slice(r0, r0 + ATTN_ROW_CHUNK)
        q = q_ref[rows, :]
        (p1, p2), l = _softmax_parts([_qkt(q, k_ref[...]), _qkt(q, kc_ref[...])])
        o = jnp.dot(p1.astype(BF16), v_ref[...], preferred_element_type=F32)
        o = o + jnp.dot(p2.astype(BF16), vc_ref[...], preferred_element_type=F32)
        o_ref[rows, :] = (o / l).astype(o_ref.dtype)


def attn_latent(q2d, k2d, v2d, kc2d, vc2d, mp, ns, nbs, past, *, heads, dq, dv, tq):
    return pl.pallas_call(
        _attn_latent_kernel,
        grid=(nbs, heads, ns // tq),
        in_specs=[
            pl.BlockSpec((tq, dq), lambda b, h, i: ((mp + b * ns) // tq + i, h)),
            pl.BlockSpec((ns, dq), lambda b, h, i: (mp // ns + b, h)),
            pl.BlockSpec((ns, dv), lambda b, h, i: (mp // ns + b, h)),
            pl.BlockSpec((past, dq), lambda b, h, i: (b, h)),
            pl.BlockSpec((past, dv), lambda b, h, i: (b, h)),
        ],
        out_specs=pl.BlockSpec((tq, dv), lambda b, h, i: (b * (ns // tq) + i, h)),
        out_shape=jax.ShapeDtypeStruct((nbs * ns, heads * dv), BF16),
        compiler_params=_params(("parallel", "parallel", "arbitrary")),
        name="attn_latent",
    )(q2d, k2d, v2d, kc2d, vc2d)


def _nat_key_row0(i, rows):
    return jnp.clip(NAT_Q_ROWS * i - NAT_WIN_ROWS // 2, 0, rows - NAT_K_ROWS)


def _nat_kernel(q_ref, k_ref, v_ref, kc_ref, vc_ref, b_ref, o_ref, *, rows):
    i = pl.program_id(2)
    start = pl.multiple_of(_nat_key_row0(i, rows) * GRID_W, GRID_W)
    nk = NAT_K_ROWS * GRID_W
    q = q_ref[...].astype(BF16)
    kw = k_ref[pl.ds(start, nk), :].astype(BF16)
    vw = v_ref[pl.ds(start, nk), :].astype(BF16)
    s_win = _qkt(q, kw) + b_ref[0, 0]
    s_ctx = _qkt(q, kc_ref[...].astype(BF16))
    (p1, p2), l = _softmax_parts([s_win, s_ctx])
    o = jnp.dot(p1.astype(BF16), vw, preferred_element_type=F32)
    o = o + jnp.dot(p2.astype(BF16), vc_ref[...].astype(BF16), preferred_element_type=F32)
    o_ref[...] = (o / l).astype(o_ref.dtype)


def _nat_bias_variants(rows):
    return (0, 1, rows // NAT_Q_ROWS - 1)


def _nat_bias(rpb, rows):
    w = GRID_W
    n_dr, n_dc = 2 * NAT_WIN_ROWS - 1, 2 * NAT_WIN_COLS - 1
    variants = _nat_bias_variants(rows)
    sel_r = np.zeros((len(variants), NAT_Q_ROWS, NAT_K_ROWS, n_dr), np.float32)
    for vi, i in enumerate(variants):
        row0 = min(max(NAT_Q_ROWS * i - NAT_WIN_ROWS // 2, 0), rows - NAT_K_ROWS)
        for a in range(NAT_Q_ROWS):
            qr = NAT_Q_ROWS * i + a
            r0 = min(max(qr - NAT_WIN_ROWS // 2, 0), rows - NAT_WIN_ROWS)
            for kl in range(NAT_K_ROWS):
                kr = row0 + kl
                if r0 <= kr < r0 + NAT_WIN_ROWS:
                    sel_r[vi, a, kl, kr - qr + NAT_WIN_ROWS - 1] = 1
    sel_c = np.zeros((n_dc, w, w), np.float32)
    for qc in range(w):
        c0 = min(max(qc - NAT_WIN_COLS // 2, 0), w - NAT_WIN_COLS)
        for kc in range(c0, c0 + NAT_WIN_COLS):
            sel_c[kc - qc + NAT_WIN_COLS - 1, qc, kc] = 1
    valid = (sel_r.sum(-1) > 0)[:, :, None, :, None] & (sel_c.sum(0) > 0)[None, None, :, None, :]
    rows_sel = jnp.einsum("hrd,vakr->hvakd", rpb.astype(F32), sel_r, precision=lax.Precision.HIGHEST)
    bias = jnp.einsum("hvakd,dqc->hvaqkc", rows_sel, sel_c, precision=lax.Precision.HIGHEST)
    bias = jnp.where(valid[None], bias * LOG2E, NEG_BIG)
    return bias.reshape(rpb.shape[0], len(variants), NAT_Q_ROWS * w, NAT_K_ROWS * w)


def nat_latent(y, kc2d, vc2d, bias, mp, ns, nbs, past):
    rows = ns // GRID_W
    tq = NAT_Q_ROWS * GRID_W
    nblk = ns // tq
    last = nblk - 1

    def bias_idx(b, h, i):
        return (h, jnp.where(i == 0, 0, jnp.where(i == last, 2, 1)), 0, 0)

    return pl.pallas_call(
        functools.partial(_nat_kernel, rows=rows),
        grid=(nbs, NAT_HEADS, nblk),
        in_specs=[
            pl.BlockSpec((tq, NAT_HD), lambda b, h, i: ((mp + b * ns) // tq + i, h)),
            pl.BlockSpec((ns, NAT_HD), lambda b, h, i: (mp // ns + b, NAT_HEADS + h)),
            pl.BlockSpec((ns, NAT_HD), lambda b, h, i: (mp // ns + b, 2 * NAT_HEADS + h)),
            pl.BlockSpec((past, NAT_HD), lambda b, h, i: (b, h)),
            pl.BlockSpec((past, NAT_HD), lambda b, h, i: (b, h)),
            pl.BlockSpec((1, 1, tq, NAT_K_ROWS * GRID_W), bias_idx),
        ],
        out_specs=pl.BlockSpec((tq, NAT_HD), lambda b, h, i: (b * nblk + i, h)),
        out_shape=jax.ShapeDtypeStruct((nbs * ns, NAT_HEADS * NAT_HD), BF16),
        compiler_params=_params(("parallel", "parallel", "arbitrary")),
        name="nat_latent",
    )(y, y, y, kc2d, vc2d, bias)


def _hgrn_tables():
    t_ = HG_TILE
    nl = int(math.log2(t_))
    nm = HG_MXU_LEVELS
    a = np.zeros((2, (nm + 1) * t_, t_), np.float32)
    for lv in range(nm):
        w = 1 << lv
        for t in range(t_):
            mid = (t // (2 * w)) * 2 * w + w
            if t >= mid:
                a[0, lv * t_ + t, mid:t + 1] = 1
            else:
                a[0, lv * t_ + t, t + 1:mid] = 1
    for t in range(t_):
        a[0, nm * t_ + t, :t + 1] = 1
    for blk in range(nm + 1):
        a[1, blk * t_:(blk + 1) * t_] = a[0, blk * t_:(blk + 1) * t_][::-1, ::-1]
    lev = np.full((2, t_, t_), -1, np.int32)
    for t in range(t_):
        for s in range(t_):
            if s == t:
                lev[:, t, s] = nl
            elif s < t:
                lev[0, t, s] = (t ^ s).bit_length() - 1
            else:
                lev[1, t, s] = (t ^ s).bit_length() - 1
    return a, lev


def _hgrn_level_sums(cum, lv, d):
    w = 1 << lv
    parts = []
    for bs in range(0, HG_TILE, 2 * w):
        mid = bs + w
        brow = mid - 1 if d == 0 else mid
        b = cum[brow:brow + 1, :]
        lo, up = cum[bs:mid], cum[mid:bs + 2 * w]
        parts += [b - lo, up - b] if d == 0 else [lo - b, b - up]
    return jnp.concatenate(parts, axis=0)


def _hgrn_tile(d, rows, cols, st, q_ref, f_ref, v_ref, lb, a_ref, lev):
    t_ = HG_TILE
    nl = int(math.log2(t_))
    nm = HG_MXU_LEVELS
    qr = q_ref[rows, cols]
    q = qr * _sigmoid(qr)
    f = lb + (1.0 - lb) * _sigmoid(f_ref[rows, cols])
    k = 1.0 - f
    g = jnp.log(f)
    v = v_ref[rows, cols].astype(BF16)
    g_hi = g.astype(BF16)
    g_lo = (g - g_hi.astype(F32)).astype(BF16)
    dd = jnp.dot(a_ref[d], jnp.concatenate([g_hi, g_lo], axis=1), preferred_element_type=F32)
    dsum = dd[:, :HG_K] + dd[:, HG_K:]
    cum = dsum[nm * t_:(nm + 1) * t_]
    att = jnp.where(lev == nl, _qkt(q.astype(BF16), k.astype(BF16)), 0.0)
    for lv in range(nl):
        e = jnp.exp(dsum[lv * t_:(lv + 1) * t_] if lv < nm else _hgrn_level_sums(cum, lv, d))
        att = jnp.where(lev == lv, _qkt((q * e).astype(BF16), (k * e).astype(BF16)), att)
    o = jnp.dot(att.astype(BF16), v, preferred_element_type=F32)
    o = o + _qkt((q * jnp.exp(cum)).astype(BF16), st.astype(BF16))
    tot_row = t_ - 1 if d == 0 else 0
    total = cum[tot_row:tot_row + 1]
    k_out = (k * jnp.exp(total - cum)).astype(BF16)
    st = st * jnp.exp(total) + lax.dot_general(v, k_out, (((0,), (0,)), ((), ())), preferred_element_type=F32)
    return o, st


def _hgrn_kernel(q_ref, f0_ref, f1_ref, v_ref, gate_ref, lb_ref, go_ref, a_ref, lev_ref, s0_ref,
                 o_ref, st_ref, o_scr, *, n, has_state):
    t_ = HG_TILE
    nt = n // t_
    o_scr[...] = jnp.zeros_like(o_scr)
    levs = (lev_ref[0], lev_ref[1])
    chains = [(hd, d) for hd in range(HG_HEADS_PER_STEP) for d in range(2)]

    def tile_group(ti, sts):
        new = []
        for (hd, d), st in zip(chains, sts):
            tile = ti if d == 0 else nt - 1 - ti
            rows = pl.ds(pl.multiple_of(tile * t_, t_), t_)
            cols = slice(hd * HG_K, (hd + 1) * HG_K)
            o, st = _hgrn_tile(d, rows, cols, st, q_ref, f1_ref if d else f0_ref, v_ref, lb_ref[d:d + 1, cols],
                               a_ref, levs[d])
            o_scr[rows, cols] += o
            new.append(st)
        return tuple(new)

    if has_state:
        st0 = tuple(s0_ref[0, d, hd].T for hd, d in chains)
    else:
        st0 = tuple(jnp.zeros((HG_V, HG_K), F32) for _ in chains)
    sts = lax.fori_loop(0, nt, tile_group, st0, unroll=HG_UNROLL)
    for (hd, d), st in zip(chains, sts):
        st_ref[0, d, hd] = st.T
    for hd in range(HG_HEADS_PER_STEP):
        cols = slice(hd * HG_V, (hd + 1) * HG_V)
        o = o_scr[:, cols]
        o = o * lax.rsqrt(jnp.mean(o * o, axis=-1, keepdims=True) + EPS) * go_ref[...]
        gate = gate_ref[:, cols]
        o_ref[:, cols] = (o * (gate * _sigmoid(gate))).astype(o_ref.dtype)


def hgrn_scan(y, lb, g_o, s0, row_blk0, nb, n, tables):
    a, lev = tables
    hps = HG_HEADS_PER_STEP
    hh = HG_HEADS // hps
    has_state = s0 is not None
    if s0 is None:
        s0 = jnp.zeros((1, 2, hps, HG_K, HG_V), F32)
        s0_spec = pl.BlockSpec((1, 2, hps, HG_K, HG_V), lambda b, h: (0, 0, 0, 0, 0))
    else:
        s0_spec = pl.BlockSpec((1, 2, hps, HG_K, HG_V), lambda b, h: (b, 0, h, 0, 0))
    col = lambda c: pl.BlockSpec((n, hps * HG_K), lambda b, h: (row_blk0 + b, c * hh + h))
    return pl.pallas_call(
        functools.partial(_hgrn_kernel, n=n, has_state=has_state),
        grid=(nb, hh),
        in_specs=[
            col(0), col(1), col(2), col(3), col(4),
            pl.BlockSpec((2, hps * HG_K), lambda b, h: (0, h)),
            pl.BlockSpec((1, HG_V), lambda b, h: (0, 0)),
            pl.BlockSpec(a.shape, lambda b, h: (0, 0, 0)),
            pl.BlockSpec(lev.shape, lambda b, h: (0, 0, 0)),
            s0_spec,
        ],
        out_specs=[
            pl.BlockSpec((n, hps * HG_V), lambda b, h: (b, h)),
            pl.BlockSpec((1, 2, hps, HG_K, HG_V), lambda b, h: (b, 0, h, 0, 0)),
        ],
        out_shape=[
            jax.ShapeDtypeStruct((nb * n, HG_HEADS * HG_V), BF16),
            jax.ShapeDtypeStruct((nb, 2, HG_HEADS, HG_K, HG_V), F32),
        ],
        scratch_shapes=[pltpu.VMEM((n, hps * HG_V), F32)],
        compiler_params=_params(("parallel", "arbitrary")),
        name="hgrn_scan",
    )(y, y, y, y, y, lb, g_o.reshape(1, HG_V), a, lev, s0)


def kernel(x_prompt, x_sample, cache_mla_ckv, cache_mla_kpe, state_hgrn, cache_nat_k, cache_nat_v, c, c_ctx, w_mod, b_mod, g_norm_mix, g_norm_ffn, w_mla_qa, g_mla_qa, w_mla_qb, g_mla_qn, w_mla_kva, g_mla_kva, w_mla_kvb, g_mla_kn, w_mla_o, w_hg_q, w_hg_f, hg_lower_bounds, w_hg_i, w_hg_g, g_hg_o, w_hg_o, w_nat_qkv, g_nat_q, g_nat_k, nat_rpb, w_nat_o, w_ff1, w_ff2):
    bp, seq, d = x_prompt.shape
    nbs, ns, _ = x_sample.shape
    past = cache_mla_ckv.shape[2]
    depth = w_mod.shape[0]
    mp, ms = bp * seq, nbs * ns
    tm = 512
    tm_proj = 1024

    x = jnp.concatenate([x_prompt.reshape(mp, d), x_sample.reshape(ms, d)], axis=0)
    cvec = jnp.zeros((8, d), F32).at[0].set(c_ctx).at[1:1 + nbs].set(c)
    mods = adaln_all(cvec, w_mod, b_mod)

    lb_all = jnp.cumsum(jax.nn.softmax(hg_lower_bounds.astype(F32), axis=1), axis=1)
    lb_all = lb_all - lb_all[:, :1]
    rope_tabs = _rope_tables(tm, ns)
    hg_tabs = _hgrn_tables()
    hg_tabs = (jnp.asarray(hg_tabs[0], BF16), jnp.asarray(hg_tabs[1]))

    w_ff1_bf, w_ff2_bf = w_ff1.astype(BF16), w_ff2.astype(BF16)

    new_ckv, new_kpe, new_hg, new_nat_k, new_nat_v = [], [], [], [], []
    for i in range(depth):
        kind, j = i % 3, i // 3
        modrows = mods[i].reshape(8 * N_MOD, 1, d)
        if kind == 0:
            w1 = jnp.concatenate(
                [_swap_tail(w_mla_kva[j]), jnp.zeros((d, MLA_Q_LORA - MLA_KV_LORA - 2 * MLA_ROPE), F32), w_mla_qa[j]],
                axis=1).astype(BF16)
            y1 = nm_matmul(x, g_norm_mix[i], modrows, 0, 1, w1, mp, ns, tm=tm_proj, tn=MLA_Q_LORA)
            w_qb_pad = _swap_tail(w_mla_qb[j].reshape(MLA_Q_LORA, MLA_HEADS, MLA_QK)).reshape(
                MLA_Q_LORA, MLA_HEADS * MLA_QK_PAD).astype(BF16)
            g_qn_pad = _swap_tail(g_mla_qn[j].reshape(1, MLA_QK)) * (MLA_QK ** -0.5 * LOG2E)
            g_kn_pad = _swap_tail(g_mla_kn[j].reshape(1, MLA_QK))
            w_kvb = w_mla_kvb[j].astype(BF16)
            q2d = mla_q(y1, g_mla_qa[j], w_qb_pad, g_qn_pad, rope_tabs, mp, ns, tm=tm)
            ckv, k2d, v2d = mla_kv(y1, 0, y1, MLA_KV_LORA // LANES, g_mla_kva[j], w_kvb, g_kn_pad, rope_tabs,
                                   mp, ns, tm=tm, normalize=True)
            ckv_c = cache_mla_ckv[:, j].reshape(nbs * past, MLA_KV_LORA)
            kpe_c = _swap_tail(cache_mla_kpe[:, j].reshape(nbs * past, MLA_ROPE))
            ident_tabs = tuple(t[:past] for t in _rope_tables(past, ns))
            _, kc2d, vc2d = mla_kv(ckv_c, 0, kpe_c, 0, g_mla_kva[j], w_kvb, g_kn_pad, ident_tabs,
                                   nbs * past, ns, tm=past, normalize=False)
            o_p = attn_block(q2d, 0, k2d, 0, v2d, 0, bp, seq, heads=MLA_HEADS, dq=MLA_QK_PAD, dv=MLA_V)
            o_s = attn_latent(q2d, k2d, v2d, kc2d, vc2d, mp, ns, nbs, past, heads=MLA_HEADS, dq=MLA_QK_PAD,
                              dv=MLA_V, tq=512)
            w_o = w_mla_o[j]
            new_ckv.append(ckv[:mp].reshape(bp, seq, MLA_KV_LORA))
            new_kpe.append(y1[:mp, MLA_KV_LORA:MLA_KV_LORA + MLA_ROPE].reshape(bp, seq, MLA_ROPE))
        elif kind == 1:
            w_all = jnp.concatenate([w_hg_q[j], w_hg_f[j, 0], w_hg_f[j, 1], w_hg_i[j], w_hg_g[j]], axis=1).astype(BF16)
            y = nm_matmul(x, g_norm_mix[i], modrows, 0, 1, w_all, mp, ns, tm=tm_proj, tn=1024)
            lb = lb_all[:, i]
            o_p, s_ctx = hgrn_scan(y, lb, g_hg_o[j], None, 0, bp, seq, hg_tabs)
            o_s, _ = hgrn_scan(y, lb, g_hg_o[j], state_hgrn[:, j], mp // ns, nbs, ns, hg_tabs)
            w_o = w_hg_o[j]
            new_hg.append(s_ctx)
        else:
            gains = jnp.concatenate([jnp.tile(g_nat_q[j] * (NAT_HD ** -0.5 * LOG2E), NAT_HEADS), jnp.tile(g_nat_k[j], NAT_HEADS),
                                     jnp.ones((d,), F32)]).reshape(1, 3 * d)
            y = nm_matmul(x, g_norm_mix[i], modrows, 0, 1, w_nat_qkv[j].astype(BF16), mp, ns, tm=tm_proj, tn=1024,
                          gains=gains, n_norm_cols=2 * d, head=NAT_HD)
            o_p = attn_block(y, 0, y, 1, y, 2, bp, seq, heads=NAT_HEADS, dq=NAT_HD, dv=NAT_HD)
            bias = _nat_bias(nat_rpb[j], ns // GRID_W)
            kc2d = cache_nat_k[:, j].reshape(nbs * past, d)
            vc2d = cache_nat_v[:, j].reshape(nbs * past, d)
            o_s = nat_latent(y, kc2d, vc2d, bias, mp, ns, nbs, past)
            w_o = w_nat_o[j]
            new_nat_k.append(y[:mp, d:2 * d].reshape(bp, seq, NAT_HEADS, NAT_HD))
            new_nat_v.append(y[:mp, 2 * d:].reshape(bp, seq, NAT_HEADS, NAT_HD))
        x = matmul_res(o_p, o_s, w_o.astype(BF16), x, modrows, 2, mp, ns, tm=tm)
        x = fused_mlp(x, g_norm_ffn[i], modrows, w_ff1_bf, w_ff2_bf, i, mp, ns, tm=tm, tf=1024)
    return (x[:mp].reshape(bp, seq, d), x[mp:].reshape(nbs, ns, d),
            jnp.stack(new_ckv, axis=1), jnp.stack(new_kpe, axis=1), jnp.stack(new_hg, axis=1),
            jnp.stack(new_nat_k, axis=1), jnp.stack(new_nat_v, axis=1))
```

```python
import functools
import math

import jax
import jax.numpy as jnp
import numpy as np
from jax import lax
from jax.experimental import pallas as pl
from jax.experimental.pallas import tpu as pltpu

F32 = jnp.float32
BF16 = jnp.bfloat16
EPS = 1e-6
ROPE_THETA = 10000.0
V7X_VMEM_LIMIT_BYTES = 56 * 1024 * 1024
LANES = 128
ROW_CHUNK = 256
ATTN_ROW_CHUNK = 256

N_MOD = 6
GRID_W = 64
MLA_HEADS, MLA_NOPE, MLA_ROPE, MLA_V = 16, 128, 64, 128
MLA_QK = MLA_NOPE + MLA_ROPE
MLA_QK_PAD = 256
MLA_Q_LORA, MLA_KV_LORA = 768, 512
HG_HEADS, HG_K, HG_V = 16, 128, 128
HG_TILE = 128
HG_UNROLL = 1
HG_MXU_LEVELS = 3
NAT_HEADS, NAT_HD = 16, 128
NAT_WIN_ROWS, NAT_WIN_COLS = 8, 16
NAT_Q_ROWS = 4
NAT_K_ROWS = 12
NEG_BIG = -1e30
LOG2E = math.log2(math.e)


def _params(sem):
    return pltpu.CompilerParams(dimension_semantics=sem, vmem_limit_bytes=V7X_VMEM_LIMIT_BYTES)


def _mod_row(i, tm, mp, ns):
    start = i * tm
    return jnp.where(start < mp, 0, 1 + (start - mp) // ns)


def _sigmoid(x):
    return 0.5 * jnp.tanh(0.5 * x) + 0.5


def _norm_modulate(x, g, shift, scale):
    y = x * lax.rsqrt(jnp.mean(x * x, axis=-1, keepdims=True) + EPS) * g
    return y * (1.0 + scale) + shift


def _adaln_kernel(c_ref, w_ref, b_ref, o_ref):
    c = c_ref[...]
    s = (c * jax.nn.sigmoid(c)).astype(BF16)
    o_ref[0] = jnp.dot(s, w_ref[0].astype(BF16), preferred_element_type=F32) + b_ref[0]


def adaln_all(cvec, w_mod, b_mod):
    depth, d, n = w_mod.shape
    rows = cvec.shape[0]
    tn = 1024
    return pl.pallas_call(
        _adaln_kernel,
        grid=(depth, n // tn),
        in_specs=[
            pl.BlockSpec((rows, d), lambda l, j: (0, 0)),
            pl.BlockSpec((1, d, tn), lambda l, j: (l, 0, j)),
            pl.BlockSpec((1, 1, tn), lambda l, j: (l, 0, j)),
        ],
        out_specs=pl.BlockSpec((1, rows, tn), lambda l, j: (l, 0, j)),
        out_shape=jax.ShapeDtypeStruct((depth, rows, n), F32),
        compiler_params=_params(("parallel", "parallel")),
        name="adaln",
    )(cvec, w_mod, b_mod.reshape(depth, 1, n))


def _by_token_group(i, n_prompt_tiles, refs, body):
    if len(refs) == 1:
        body(refs[0])
    else:
        pl.when(i < n_prompt_tiles)(lambda: body(refs[0]))
        pl.when(i >= n_prompt_tiles)(lambda: body(refs[1]))


def _token_group_specs(block, n_prompt_tiles, n_src):
    if n_src == 1:
        return [pl.BlockSpec(block, lambda i, *_: (i, 0))]
    return [pl.BlockSpec(block, lambda i, *_: (jnp.minimum(i, n_prompt_tiles - 1), 0)),
            pl.BlockSpec(block, lambda i, *_: (jnp.maximum(i - n_prompt_tiles, 0), 0))]


def _nm_matmul_kernel(*refs, n_x, n_prompt_tiles, n_norm_tiles, head):
    x_refs = refs[:n_x]
    g_ref, sh_ref, sc_ref, w_ref, gain_ref, o_ref, h_scr = refs[n_x:]
    i, j = pl.program_id(0), pl.program_id(1)
    tm = h_scr.shape[0]

    def emit(rows, h, normed):
        acc = jnp.dot(h, w_ref[...], preferred_element_type=F32)
        if not normed:
            o_ref[rows, :] = acc
            return
        for c in range(acc.shape[1] // head):
            cols = slice(c * head, (c + 1) * head)
            blk = acc[:, cols]
            r = lax.rsqrt(jnp.mean(blk * blk, axis=-1, keepdims=True) + EPS)
            o_ref[rows, cols] = blk * r * gain_ref[:, cols]

    def first(x_ref):
        for r0 in range(0, tm, ROW_CHUNK):
            rows = slice(r0, r0 + ROW_CHUNK)
            h = _norm_modulate(x_ref[rows, :], g_ref[...], sh_ref[0], sc_ref[0]).astype(BF16)
            h_scr[rows, :] = h
            emit(rows, h, n_norm_tiles > 0)

    pl.when(j == 0)(lambda: _by_token_group(i, n_prompt_tiles, x_refs, first))
    if n_norm_tiles > 1:
        pl.when((j > 0) & (j < n_norm_tiles))(lambda: emit(slice(None), h_scr[...], True))
    pl.when(j >= max(n_norm_tiles, 1))(lambda: emit(slice(None), h_scr[...], False))


def nm_matmul(xs, g, modrows, k_shift, k_scale, w, mp, ns, *, tm, tn, gains=None, n_norm_cols=0, head=LANES):
    xs = xs if isinstance(xs, tuple) else (xs,)
    m, d = sum(x.shape[0] for x in xs), xs[0].shape[1]
    n = w.shape[1]
    if gains is None:
        gains = jnp.ones((1, n), F32)
    row = functools.partial(_mod_row, tm=tm, mp=mp, ns=ns)
    return pl.pallas_call(
        functools.partial(_nm_matmul_kernel, n_x=len(xs), n_prompt_tiles=mp // tm, n_norm_tiles=n_norm_cols // tn,
                          head=head),
        grid=(m // tm, n // tn),
        in_specs=_token_group_specs((tm, d), mp // tm, len(xs)) + [
            pl.BlockSpec((1, d), lambda i, j: (0, 0)),
            pl.BlockSpec((1, 1, d), lambda i, j: (row(i) * N_MOD + k_shift, 0, 0)),
            pl.BlockSpec((1, 1, d), lambda i, j: (row(i) * N_MOD + k_scale, 0, 0)),
            pl.BlockSpec((d, tn), lambda i, j: (0, j)),
            pl.BlockSpec((1, tn), lambda i, j: (0, j)),
        ],
        out_specs=pl.BlockSpec((tm, tn), lambda i, j: (i, j)),
        out_shape=jax.ShapeDtypeStruct((m, n), F32),
        scratch_shapes=[pltpu.VMEM((tm, d), BF16)],
        compiler_params=_params(("parallel", "arbitrary")),
        name="nm_matmul",
    )(*xs, g.reshape(1, d), modrows, modrows, w, gains)


def _matmul_res_kernel(*refs, n_x, n_prompt_tiles):
    ap_ref, as_ref, w_ref = refs[:3]
    x_refs = refs[3:3 + n_x]
    gt_ref, o_ref = refs[3 + n_x:]

    def finish(a_ref, x_ref):
        acc = jnp.dot(a_ref[...], w_ref[...], preferred_element_type=F32)
        o_ref[...] = x_ref[...] + gt_ref[0] * acc

    i = pl.program_id(0)
    pl.when(i < n_prompt_tiles)(lambda: finish(ap_ref, x_refs[0]))
    pl.when(i >= n_prompt_tiles)(lambda: finish(as_ref, x_refs[-1]))


def matmul_res(a_p, a_s, w, xs, modrows, k_gate, mp, ns, *, tm):
    xs = xs if isinstance(xs, tuple) else (xs,)
    kdim = a_p.shape[1]
    m = a_p.shape[0] + a_s.shape[0]
    n = w.shape[1]
    npt = mp // tm
    row = functools.partial(_mod_row, tm=tm, mp=mp, ns=ns)
    return pl.pallas_call(
        functools.partial(_matmul_res_kernel, n_x=len(xs), n_prompt_tiles=npt),
        grid=(m // tm,),
        in_specs=_token_group_specs((tm, kdim), npt, 2) + [pl.BlockSpec((kdim, n), lambda i: (0, 0))]
        + _token_group_specs((tm, n), npt, len(xs))
        + [pl.BlockSpec((1, 1, n), lambda i: (row(i) * N_MOD + k_gate, 0, 0))],
        out_specs=pl.BlockSpec((tm, n), lambda i: (i, 0)),
        out_shape=jax.ShapeDtypeStruct((m, n), F32),
        compiler_params=_params(("parallel",)),
        name="matmul_res",
    )(a_p, a_s, w, *xs, modrows)


def _mlp_kernel(x_ref, g_ref, sh_ref, sc_ref, gt_ref, w1_ref, w2_ref, o_ref, h_scr, acc_scr):
    f = pl.program_id(1)
    tm = x_ref.shape[0]

    def hidden(h):
        u = jnp.dot(h, w1_ref[...], preferred_element_type=F32)
        u = jnp.square(jnp.maximum(u, 0.0)).astype(BF16)
        return jnp.dot(u, w2_ref[...], preferred_element_type=F32)

    @pl.when(f == 0)
    def _():
        for r0 in range(0, tm, ROW_CHUNK):
            rows = slice(r0, r0 + ROW_CHUNK)
            h = _norm_modulate(x_ref[rows, :], g_ref[...], sh_ref[0], sc_ref[0]).astype(BF16)
            h_scr[rows, :] = h
            acc_scr[rows, :] = hidden(h)

    @pl.when(f > 0)
    def _():
        acc_scr[...] += hidden(h_scr[...])

    @pl.when(f == pl.num_programs(1) - 1)
    def _():
        o_ref[...] = x_ref[...] + gt_ref[0] * acc_scr[...]


def fused_mlp(x, g, modrows, w1, w2, layer, mp, ns, *, tm, tf, row0=0, n_rows=None):
    d = x.shape[1]
    m = x.shape[0] - row0 if n_rows is None else n_rows
    t0 = row0 // tm
    ff = w1.shape[2]
    row = functools.partial(_mod_row, tm=tm, mp=mp, ns=ns)
    modspec = lambda k: pl.BlockSpec((1, 1, d), lambda i, f: (row(i + t0) * N_MOD + k, 0, 0))
    return pl.pallas_call(
        _mlp_kernel,
        grid=(m // tm, ff // tf),
        in_specs=[
            pl.BlockSpec((tm, d), lambda i, f: (i + t0, 0)),
            pl.BlockSpec((1, d), lambda i, f: (0, 0)),
            modspec(3), modspec(4), modspec(5),
            pl.BlockSpec((None, d, tf), lambda i, f: (layer, 0, f)),
            pl.BlockSpec((None, tf, d), lambda i, f: (layer, f, 0)),
        ],
        out_specs=pl.BlockSpec((tm, d), lambda i, f: (i, 0)),
        out_shape=jax.ShapeDtypeStruct((m, d), F32),
        scratch_shapes=[pltpu.VMEM((tm, d), BF16), pltpu.VMEM((tm, d), F32)],
        compiler_params=_params(("parallel", "arbitrary")),
        name="fused_mlp",
    )(x, g.reshape(1, d), modrows, modrows, modrows, w1, w2)


def _rope_lanes(x, c_ref, s_ref):
    return x * c_ref[...] + pltpu.roll(x, LANES // 2, 1) * s_ref[...]


def _swap_tail(x):
    half = MLA_ROPE // 2
    pe = x[..., -MLA_ROPE:]
    return jnp.concatenate([x, pe[..., half:], pe[..., :half]], axis=-1)


def _mla_q_kernel(qa_ref, gqa_ref, w_ref, gqn_ref, c_ref, s_ref, o_ref):
    qa = qa_ref[...]
    qa = (qa * lax.rsqrt(jnp.mean(qa * qa, axis=-1, keepdims=True) + EPS) * gqa_ref[...]).astype(BF16)
    q = jnp.dot(qa, w_ref[...], preferred_element_type=F32)
    for h in range(MLA_HEADS):
        nope = q[:, h * MLA_QK_PAD:h * MLA_QK_PAD + LANES]
        pe = q[:, h * MLA_QK_PAD + LANES:(h + 1) * MLA_QK_PAD]
        ss = jnp.sum(nope * nope + 0.5 * (pe * pe), axis=-1, keepdims=True)
        r = lax.rsqrt(ss / MLA_QK + EPS)
        o_ref[:, h * MLA_QK_PAD:h * MLA_QK_PAD + LANES] = (nope * r * gqn_ref[:, :LANES]).astype(BF16)
        o_ref[:, h * MLA_QK_PAD + LANES:(h + 1) * MLA_QK_PAD] = _rope_lanes(
            pe * r * gqn_ref[:, LANES:], c_ref, s_ref).astype(BF16)


def _rope_spec(tm, mp, ns):
    def idx(i):
        start = i * tm
        return (jnp.where(start < mp, 0, 1 + ((start - mp) % ns) // tm), 0)

    return pl.BlockSpec((tm, LANES), idx)


def mla_q(y1, g_qa, w_qb_pad, g_qn_pad, tables, mp, ns, *, tm):
    m = y1.shape[0]
    n = MLA_HEADS * MLA_QK_PAD
    rs = _rope_spec(tm, mp, ns)
    return pl.pallas_call(
        _mla_q_kernel,
        grid=(m // tm,),
        in_specs=[
            pl.BlockSpec((tm, MLA_Q_LORA), lambda i: (i, 1)),
            pl.BlockSpec((1, MLA_Q_LORA), lambda i: (0, 0)),
            pl.BlockSpec((MLA_Q_LORA, n), lambda i: (0, 0)),
            pl.BlockSpec((1, MLA_QK_PAD), lambda i: (0, 0)),
            rs, rs,
        ],
        out_specs=pl.BlockSpec((tm, n), lambda i: (i, 0)),
        out_shape=jax.ShapeDtypeStruct((m, n), BF16),
        compiler_params=_params(("parallel",)),
        name="mla_q",
    )(y1, g_qa.reshape(1, MLA_Q_LORA), w_qb_pad, g_qn_pad, *tables)


def _mla_kv_kernel(ckv_ref, kpe_ref, gkva_ref, w_ref, gkn_ref, c_ref, s_ref,
                   ckv_out_ref, k_ref, v_ref, *, normalize):
    ckv = ckv_ref[...]
    if normalize:
        ckv = ckv * lax.rsqrt(jnp.mean(ckv * ckv, axis=-1, keepdims=True) + EPS) * gkva_ref[...]
    ckv_out_ref[...] = ckv
    kv = jnp.dot(ckv.astype(BF16), w_ref[...], preferred_element_type=F32)
    kpe = kpe_ref[...]
    pe_ss = 0.5 * jnp.sum(kpe * kpe, axis=-1, keepdims=True)
    pe_rot = _rope_lanes(kpe * gkn_ref[:, LANES:], c_ref, s_ref)
    for h in range(MLA_HEADS):
        nope = kv[:, h * 2 * LANES:h * 2 * LANES + LANES]
        r = lax.rsqrt((jnp.sum(nope * nope, axis=-1, keepdims=True) + pe_ss) / MLA_QK + EPS)
        k_ref[:, h * MLA_QK_PAD:h * MLA_QK_PAD + LANES] = (nope * r * gkn_ref[:, :LANES]).astype(BF16)
        k_ref[:, h * MLA_QK_PAD + LANES:(h + 1) * MLA_QK_PAD] = (pe_rot * r).astype(BF16)
        v_ref[:, h * MLA_V:(h + 1) * MLA_V] = kv[:, h * 2 * LANES + LANES:(h + 1) * 2 * LANES].astype(BF16)


def mla_kv(ckv_src, ckv_col, kpe_src, kpe_col, g_kva, w_kvb, g_kn_pad, tables, mp, ns, *, tm, normalize):
    m = ckv_src.shape[0]
    rs = _rope_spec(tm, mp, ns)
    return pl.pallas_call(
        functools.partial(_mla_kv_kernel, normalize=normalize),
        grid=(m // tm,),
        in_specs=[
            pl.BlockSpec((tm, MLA_KV_LORA), lambda i: (i, ckv_col)),
            pl.BlockSpec((tm, LANES), lambda i: (i, kpe_col)),
            pl.BlockSpec((1, MLA_KV_LORA), lambda i: (0, 0)),
            pl.BlockSpec((MLA_KV_LORA, MLA_HEADS * 2 * LANES), lambda i: (0, 0)),
            pl.BlockSpec((1, MLA_QK_PAD), lambda i: (0, 0)),
            rs, rs,
        ],
        out_specs=[
            pl.BlockSpec((tm, MLA_KV_LORA), lambda i: (i, 0)),
            pl.BlockSpec((tm, MLA_HEADS * MLA_QK_PAD), lambda i: (i, 0)),
            pl.BlockSpec((tm, MLA_HEADS * MLA_V), lambda i: (i, 0)),
        ],
        out_shape=[
            jax.ShapeDtypeStruct((m, MLA_KV_LORA), F32),
            jax.ShapeDtypeStruct((m, MLA_HEADS * MLA_QK_PAD), BF16),
            jax.ShapeDtypeStruct((m, MLA_HEADS * MLA_V), BF16),
        ],
        compiler_params=_params(("parallel",)),
        name="mla_kv",
    )(ckv_src, kpe_src, g_kva.reshape(1, MLA_KV_LORA), w_kvb, g_kn_pad, *tables)


def _rope_tables(tm, ns):
    n_freq = MLA_ROPE // 4
    inv = 1.0 / (ROPE_THETA ** (jnp.arange(n_freq, dtype=F32) / n_freq))
    t = jnp.arange(ns)
    row = (t // GRID_W).astype(F32)
    col = (t % GRID_W).astype(F32)
    ang = jnp.concatenate([row[:, None] * inv, col[:, None] * inv], axis=-1)
    cos, sin = jnp.cos(ang), jnp.sin(ang)
    z = jnp.zeros((ns, LANES - MLA_ROPE), F32)
    c = jnp.concatenate([cos, cos, z], axis=-1)
    s = jnp.concatenate([-sin, sin, z], axis=-1)
    ident = jnp.concatenate([jnp.ones((tm, MLA_ROPE), F32), jnp.zeros((tm, LANES - MLA_ROPE), F32)], axis=-1)
    return jnp.concatenate([ident, c], axis=0), jnp.concatenate([jnp.zeros((tm, LANES), F32), s], axis=0)


def _softmax_parts(scores):
    m = scores[0].max(axis=-1, keepdims=True)
    for s in scores[1:]:
        m = jnp.maximum(m, s.max(axis=-1, keepdims=True))
    ps = [jnp.exp2(s - m) for s in scores]
    l = ps[0].sum(axis=-1, keepdims=True)
    for p in ps[1:]:
        l = l + p.sum(axis=-1, keepdims=True)
    return ps, l


def _qkt(q, k):
    return lax.dot_general(q, k, (((1,), (1,)), ((), ())), preferred_element_type=F32)


def _attn_block_kernel(q_ref, k_ref, v_ref, o_ref, *, heads, dq, dv):
    for h in range(heads):
        q = q_ref[:, h * dq:(h + 1) * dq].astype(BF16)
        k = k_ref[:, h * dq:(h + 1) * dq].astype(BF16)
        v = v_ref[:, h * dv:(h + 1) * dv].astype(BF16)
        (p,), l = _softmax_parts([_qkt(q, k)])
        o = jnp.dot(p.astype(BF16), v, preferred_element_type=F32) / l
        o_ref[:, h * dv:(h + 1) * dv] = o.astype(o_ref.dtype)


def attn_block(q_src, q_col, k_src, k_col, v_src, v_col, nb, seq, *, heads, dq, dv):
    return pl.pallas_call(
        functools.partial(_attn_block_kernel, heads=heads, dq=dq, dv=dv),
        grid=(nb,),
        in_specs=[
            pl.BlockSpec((seq, heads * dq), lambda b: (b, q_col)),
            pl.BlockSpec((seq, heads * dq), lambda b: (b, k_col)),
            pl.BlockSpec((seq, heads * dv), lambda b: (b, v_col)),
        ],
        out_specs=pl.BlockSpec((seq, heads * dv), lambda b: (b, 0)),
        out_shape=jax.ShapeDtypeStruct((nb * seq, heads * dv), BF16),
        compiler_params=_params(("parallel",)),
        name="attn_block",
    )(q_src, k_src, v_src)


def _attn_latent_kernel(q_ref, k_ref, v_ref, kc_ref, vc_ref, o_ref):
    for r0 in range(0, q_ref.shape[0], ATTN_ROW_CHUNK):
        rows = slice(r0, r0 + ATTN_ROW_CHUNK)
        q = q_ref[rows, :]
        (p1, p2), l = _softmax_parts([_qkt(q, k_ref[...]), _qkt(q, kc_ref[...])])
        o = jnp.dot(p1.astype(BF16), v_ref[...], preferred_element_type=F32)
        o = o + jnp.dot(p2.astype(BF16), vc_ref[...], preferred_element_type=F32)
        o_ref[rows, :] = (o / l).astype(o_ref.dtype)


def attn_latent(q2d, k2d, v2d, kc2d, vc2d, mp, ns, nbs, past, *, heads, dq, dv, tq):
    return pl.pallas_call(
        _attn_latent_kernel,
        grid=(nbs, heads, ns // tq),
        in_specs=[
            pl.BlockSpec((tq, dq), lambda b, h, i: ((mp + b * ns) // tq + i, h)),
            pl.BlockSpec((ns, dq), lambda b, h, i: (mp // ns + b, h)),
            pl.BlockSpec((ns, dv), lambda b, h, i: (mp // ns + b, h)),
            pl.BlockSpec((past, dq), lambda b, h, i: (b, h)),
            pl.BlockSpec((past, dv), lambda b, h, i: (b, h)),
        ],
        out_specs=pl.BlockSpec((tq, dv), lambda b, h, i: (b * (ns // tq) + i, h)),
        out_shape=jax.ShapeDtypeStruct((nbs * ns, heads * dv), BF16),
        compiler_params=_params(("parallel", "parallel", "arbitrary")),
        name="attn_latent",
    )(q2d, k2d, v2d, kc2d, vc2d)


def _nat_key_row0(i, rows):
    return jnp.clip(NAT_Q_ROWS * i - NAT_WIN_ROWS // 2, 0, rows - NAT_K_ROWS)


def _nat_kernel(q_ref, k_ref, v_ref, kc_ref, vc_ref, b_ref, o_ref, *, rows):
    i = pl.program_id(2)
    start = pl.multiple_of(_nat_key_row0(i, rows) * GRID_W, GRID_W)
    nk = NAT_K_ROWS * GRID_W
    q = q_ref[...].astype(BF16)
    kw = k_ref[pl.ds(start, nk), :].astype(BF16)
    vw = v_ref[pl.ds(start, nk), :].astype(BF16)
    s_win = _qkt(q, kw) + b_ref[0, 0]
    s_ctx = _qkt(q, kc_ref[...].astype(BF16))
    (p1, p2), l = _softmax_parts([s_win, s_ctx])
    o = jnp.dot(p1.astype(BF16), vw, preferred_element_type=F32)
    o = o + jnp.dot(p2.astype(BF16), vc_ref[...].astype(BF16), preferred_element_type=F32)
    o_ref[...] = (o / l).astype(o_ref.dtype)


def _nat_bias_variants(rows):
    return (0, 1, rows // NAT_Q_ROWS - 1)


def _nat_bias(rpb, rows):
    w = GRID_W
    n_dr, n_dc = 2 * NAT_WIN_ROWS - 1, 2 * NAT_WIN_COLS - 1
    variants = _nat_bias_variants(rows)
    sel_r = np.zeros((len(variants), NAT_Q_ROWS, NAT_K_ROWS, n_dr), np.float32)
    for vi, i in enumerate(variants):
        row0 = min(max(NAT_Q_ROWS * i - NAT_WIN_ROWS // 2, 0), rows - NAT_K_ROWS)
        for a in range(NAT_Q_ROWS):
            qr = NAT_Q_ROWS * i + a
            r0 = min(max(qr - NAT_WIN_ROWS // 2, 0), rows - NAT_WIN_ROWS)
            for kl in range(NAT_K_ROWS):
                kr = row0 + kl
                if r0 <= kr < r0 + NAT_WIN_ROWS:
                    sel_r[vi, a, kl, kr - qr + NAT_WIN_ROWS - 1] = 1
    sel_c = np.zeros((n_dc, w, w), np.float32)
    for qc in range(w):
        c0 = min(max(qc - NAT_WIN_COLS // 2, 0), w - NAT_WIN_COLS)
        for kc in range(c0, c0 + NAT_WIN_COLS):
            sel_c[kc - qc + NAT_WIN_COLS - 1, qc, kc] = 1
    valid = (sel_r.sum(-1) > 0)[:, :, None, :, None] & (sel_c.sum(0) > 0)[None, None, :, None, :]
    rows_sel = jnp.einsum("hrd,vakr->hvakd", rpb.astype(F32), sel_r, precision=lax.Precision.HIGHEST)
    bias = jnp.einsum("hvakd,dqc->hvaqkc", rows_sel, sel_c, precision=lax.Precision.HIGHEST)
    bias = jnp.where(valid[None], bias * LOG2E, NEG_BIG)
    return bias.reshape(rpb.shape[0], len(variants), NAT_Q_ROWS * w, NAT_K_ROWS * w)


def nat_latent(y, kc2d, vc2d, bias, mp, ns, nbs, past):
    rows = ns // GRID_W
    tq = NAT_Q_ROWS * GRID_W
    nblk = ns // tq
    last = nblk - 1

    def bias_idx(b, h, i):
        return (h, jnp.where(i == 0, 0, jnp.where(i == last, 2, 1)), 0, 0)

    return pl.pallas_call(
        functools.partial(_nat_kernel, rows=rows),
        grid=(nbs, NAT_HEADS, nblk),
        in_specs=[
            pl.BlockSpec((tq, NAT_HD), lambda b, h, i: ((mp + b * ns) // tq + i, h)),
            pl.BlockSpec((ns, NAT_HD), lambda b, h, i: (mp // ns + b, NAT_HEADS + h)),
            pl.BlockSpec((ns, NAT_HD), lambda b, h, i: (mp // ns + b, 2 * NAT_HEADS + h)),
            pl.BlockSpec((past, NAT_HD), lambda b, h, i: (b, h)),
            pl.BlockSpec((past, NAT_HD), lambda b, h, i: (b, h)),
            pl.BlockSpec((1, 1, tq, NAT_K_ROWS * GRID_W), bias_idx),
        ],
        out_specs=pl.BlockSpec((tq, NAT_HD), lambda b, h, i: (b * nblk + i, h)),
        out_shape=jax.ShapeDtypeStruct((nbs * ns, NAT_HEADS * NAT_HD), BF16),
        compiler_params=_params(("parallel", "parallel", "arbitrary")),
        name="nat_latent",
    )(y, y, y, kc2d, vc2d, bias)


def _hgrn_tables():
    t_ = HG_TILE
    nl = int(math.log2(t_))
    nm = HG_MXU_LEVELS
    a = np.zeros((2, (nm + 1) * t_, t_), np.float32)
    for lv in range(nm):
        w = 1 << lv
        for t in range(t_):
            mid = (t // (2 * w)) * 2 * w + w
            if t >= mid:
                a[0, lv * t_ + t, mid:t + 1] = 1
            else:
                a[0, lv * t_ + t, t + 1:mid] = 1
    for t in range(t_):
        a[0, nm * t_ + t, :t + 1] = 1
    for blk in range(nm + 1):
        a[1, blk * t_:(blk + 1) * t_] = a[0, blk * t_:(blk + 1) * t_][::-1, ::-1]
    lev = np.full((2, t_, t_), -1, np.int32)
    for t in range(t_):
        for s in range(t_):
            if s == t:
                lev[:, t, s] = nl
            elif s < t:
                lev[0, t, s] = (t ^ s).bit_length() - 1
            else:
                lev[1, t, s] = (t ^ s).bit_length() - 1
    return a, lev


def _hgrn_level_sums(cum, lv, d):
    w = 1 << lv
    parts = []
    for bs in range(0, HG_TILE, 2 * w):
        mid = bs + w
        brow = mid - 1 if d == 0 else mid
        b = cum[brow:brow + 1, :]
        lo, up = cum[bs:mid], cum[mid:bs + 2 * w]
        parts += [b - lo, up - b] if d == 0 else [lo - b, b - up]
    return jnp.concatenate(parts, axis=0)


def _hgrn_tile(d, rows, cols, st, q_ref, f_ref, v_ref, lb, a_ref, lev):
    t_ = HG_TILE
    nl = int(math.log2(t_))
    nm = HG_MXU_LEVELS
    qr = q_ref[rows, cols]
    q = qr * _sigmoid(qr)
    f = lb + (1.0 - lb) * _sigmoid(f_ref[rows, cols])
    k = 1.0 - f
    g = jnp.log(f)
    v = v_ref[rows, cols].astype(BF16)
    g_hi = g.astype(BF16)
    g_lo = (g - g_hi.astype(F32)).astype(BF16)
    dd = jnp.dot(a_ref[d], jnp.concatenate([g_hi, g_lo], axis=1), preferred_element_type=F32)
    dsum = dd[:, :HG_K] + dd[:, HG_K:]
    cum = dsum[nm * t_:(nm + 1) * t_]
    att = jnp.where(lev == nl, _qkt(q.astype(BF16), k.astype(BF16)), 0.0)
    for lv in range(nl):
        e = jnp.exp(dsum[lv * t_:(lv + 1) * t_] if lv < nm else _hgrn_level_sums(cum, lv, d))
        att = jnp.where(lev == lv, _qkt((q * e).astype(BF16), (k * e).astype(BF16)), att)
    o = jnp.dot(att.astype(BF16), v, preferred_element_type=F32)
    o = o + _qkt((q * jnp.exp(cum)).astype(BF16), st.astype(BF16))
    tot_row = t_ - 1 if d == 0 else 0
    total = cum[tot_row:tot_row + 1]
    k_out = (k * jnp.exp(total - cum)).astype(BF16)
    st = st * jnp.exp(total) + lax.dot_general(v, k_out, (((0,), (0,)), ((), ())), preferred_element_type=F32)
    return o, st


def _hgrn_kernel(q_ref, f0_ref, f1_ref, v_ref, gate_ref, lb_ref, go_ref, a_ref, lev_ref, s0_ref,
                 o_ref, st_ref, o_scr, *, n, has_state):
    t_ = HG_TILE
    nt = n // t_
    o_scr[...] = jnp.zeros_like(o_scr)
    levs = (lev_ref[0], lev_ref[1])
    hps = q_ref.shape[1] // HG_K
    chains = [(hd, d) for hd in range(hps) for d in range(2)]

    def tile_group(ti, sts):
        new = []
        for (hd, d), st in zip(chains, sts):
            tile = ti if d == 0 else nt - 1 - ti
            rows = pl.ds(pl.multiple_of(tile * t_, t_), t_)
            cols = slice(hd * HG_K, (hd + 1) * HG_K)
            o, st = _hgrn_tile(d, rows, cols, st, q_ref, f1_ref if d else f0_ref, v_ref, lb_ref[d:d + 1, cols],
                               a_ref, levs[d])
            o_scr[rows, cols] += o
            new.append(st)
        return tuple(new)

    if has_state:
        st0 = tuple(s0_ref[0, d, hd].T for hd, d in chains)
    else:
        st0 = tuple(jnp.zeros((HG_V, HG_K), F32) for _ in chains)
    sts = lax.fori_loop(0, nt, tile_group, st0, unroll=HG_UNROLL)
    for (hd, d), st in zip(chains, sts):
        st_ref[0, d, hd] = st.T
    for hd in range(hps):
        cols = slice(hd * HG_V, (hd + 1) * HG_V)
        o = o_scr[:, cols]
        o = o * lax.rsqrt(jnp.mean(o * o, axis=-1, keepdims=True) + EPS) * go_ref[...]
        gate = gate_ref[:, cols]
        o_ref[:, cols] = (o * (gate * _sigmoid(gate))).astype(o_ref.dtype)


def hgrn_scan(y, lb, g_o, s0, row_blk0, nb, n, tables, *, hps):
    a, lev = tables
    hh = HG_HEADS // hps
    has_state = s0 is not None
    if s0 is None:
        s0 = jnp.zeros((1, 2, hps, HG_K, HG_V), F32)
        s0_spec = pl.BlockSpec((1, 2, hps, HG_K, HG_V), lambda b, h: (0, 0, 0, 0, 0))
    else:
        s0_spec = pl.BlockSpec((1, 2, hps, HG_K, HG_V), lambda b, h: (b, 0, h, 0, 0))
    col = lambda c: pl.BlockSpec((n, hps * HG_K), lambda b, h: (row_blk0 + b, c * hh + h))
    return pl.pallas_call(
        functools.partial(_hgrn_kernel, n=n, has_state=has_state),
        grid=(nb, hh),
        in_specs=[
            col(0), col(1), col(2), col(3), col(4),
            pl.BlockSpec((2, hps * HG_K), lambda b, h: (0, h)),
            pl.BlockSpec((1, HG_V), lambda b, h: (0, 0)),
            pl.BlockSpec(a.shape, lambda b, h: (0, 0, 0)),
            pl.BlockSpec(lev.shape, lambda b, h: (0, 0, 0)),
            s0_spec,
        ],
        out_specs=[
            pl.BlockSpec((n, hps * HG_V), lambda b, h: (b, h)),
            pl.BlockSpec((1, 2, hps, HG_K, HG_V), lambda b, h: (b, 0, h, 0, 0)),
        ],
        out_shape=[
            jax.ShapeDtypeStruct((nb * n, HG_HEADS * HG_V), BF16),
            jax.ShapeDtypeStruct((nb, 2, HG_HEADS, HG_K, HG_V), F32),
        ],
        scratch_shapes=[pltpu.VMEM((n, hps * HG_V), F32)],
        compiler_params=_params(("parallel", "arbitrary")),
        name="hgrn_scan",
    )(y, y, y, y, y, lb, g_o.reshape(1, HG_V), a, lev, s0)


def kernel(x_prompt, x_sample, cache_mla_ckv, cache_mla_kpe, state_hgrn, cache_nat_k, cache_nat_v, c, c_ctx, w_mod, b_mod, g_norm_mix, g_norm_ffn, w_mla_qa, g_mla_qa, w_mla_qb, g_mla_qn, w_mla_kva, g_mla_kva, w_mla_kvb, g_mla_kn, w_mla_o, w_hg_q, w_hg_f, hg_lower_bounds, w_hg_i, w_hg_g, g_hg_o, w_hg_o, w_nat_qkv, g_nat_q, g_nat_k, nat_rpb, w_nat_o, w_ff1, w_ff2):
    bp, seq, d = x_prompt.shape
    nbs, ns, _ = x_sample.shape
    past = cache_mla_ckv.shape[2]
    depth = w_mod.shape[0]
    mp, ms = bp * seq, nbs * ns
    tm = 512
    tm_proj = 1024

    x = (x_prompt.reshape(mp, d), x_sample.reshape(ms, d))
    cvec = jnp.zeros((8, d), F32).at[0].set(c_ctx).at[1:1 + nbs].set(c)
    mods = adaln_all(cvec, w_mod, b_mod)

    lb_all = jnp.cumsum(jax.nn.softmax(hg_lower_bounds.astype(F32), axis=1), axis=1)
    lb_all = lb_all - lb_all[:, :1]
    rope_tabs = _rope_tables(tm, ns)
    hg_tabs = _hgrn_tables()
    hg_tabs = (jnp.asarray(hg_tabs[0], BF16), jnp.asarray(hg_tabs[1]))

    w_ff1_bf, w_ff2_bf = w_ff1.astype(BF16), w_ff2.astype(BF16)

    new_ckv, new_kpe, new_hg, new_nat_k, new_nat_v = [], [], [], [], []
    for i in range(depth):
        kind, j = i % 3, i // 3
        modrows = mods[i].reshape(8 * N_MOD, 1, d)
        if kind == 0:
            w1 = jnp.concatenate(
                [_swap_tail(w_mla_kva[j]), jnp.zeros((d, MLA_Q_LORA - MLA_KV_LORA - 2 * MLA_ROPE), F32), w_mla_qa[j]],
                axis=1).astype(BF16)
            y1 = nm_matmul(x, g_norm_mix[i], modrows, 0, 1, w1, mp, ns, tm=tm if isinstance(x, tuple) else tm_proj,
                           tn=MLA_Q_LORA)
            w_qb_pad = _swap_tail(w_mla_qb[j].reshape(MLA_Q_LORA, MLA_HEADS, MLA_QK)).reshape(
                MLA_Q_LORA, MLA_HEADS * MLA_QK_PAD).astype(BF16)
            g_qn_pad = _swap_tail(g_mla_qn[j].reshape(1, MLA_QK)) * (MLA_QK ** -0.5 * LOG2E)
            g_kn_pad = _swap_tail(g_mla_kn[j].reshape(1, MLA_QK))
            w_kvb = w_mla_kvb[j].astype(BF16)
            q2d = mla_q(y1, g_mla_qa[j], w_qb_pad, g_qn_pad, rope_tabs, mp, ns, tm=tm)
            ckv, k2d, v2d = mla_kv(y1, 0, y1, MLA_KV_LORA // LANES, g_mla_kva[j], w_kvb, g_kn_pad, rope_tabs,
                                   mp, ns, tm=tm, normalize=True)
            ckv_c = cache_mla_ckv[:, j].reshape(nbs * past, MLA_KV_LORA)
            kpe_c = _swap_tail(cache_mla_kpe[:, j].reshape(nbs * past, MLA_ROPE))
            ident_tabs = tuple(t[:past] for t in _rope_tables(past, ns))
            _, kc2d, vc2d = mla_kv(ckv_c, 0, kpe_c, 0, g_mla_kva[j], w_kvb, g_kn_pad, ident_tabs,
                                   nbs * past, ns, tm=past, normalize=False)
            o_p = attn_block(q2d, 0, k2d, 0, v2d, 0, bp, seq, heads=MLA_HEADS, dq=MLA_QK_PAD, dv=MLA_V)
            o_s = attn_latent(q2d, k2d, v2d, kc2d, vc2d, mp, ns, nbs, past, heads=MLA_HEADS, dq=MLA_QK_PAD,
                              dv=MLA_V, tq=1024)
            w_o = w_mla_o[j]
            new_ckv.append(ckv[:mp].reshape(bp, seq, MLA_KV_LORA))
            new_kpe.append(y1[:mp, MLA_KV_LORA:MLA_KV_LORA + MLA_ROPE].reshape(bp, seq, MLA_ROPE))
        elif kind == 1:
            w_all = jnp.concatenate([w_hg_q[j], w_hg_f[j, 0], w_hg_f[j, 1], w_hg_i[j], w_hg_g[j]], axis=1).astype(BF16)
            y = nm_matmul(x, g_norm_mix[i], modrows, 0, 1, w_all, mp, ns, tm=tm_proj, tn=1024)
            lb = lb_all[:, i]
            o_p, s_ctx = hgrn_scan(y, lb, g_hg_o[j], None, 0, bp, seq, hg_tabs, hps=4)
            o_s, _ = hgrn_scan(y, lb, g_hg_o[j], state_hgrn[:, j], mp // ns, nbs, ns, hg_tabs, hps=2)
            w_o = w_hg_o[j]
            new_hg.append(s_ctx)
        else:
            gains = jnp.concatenate([jnp.tile(g_nat_q[j] * (NAT_HD ** -0.5 * LOG2E), NAT_HEADS), jnp.tile(g_nat_k[j], NAT_HEADS),
                                     jnp.ones((d,), F32)]).reshape(1, 3 * d)
            y = nm_matmul(x, g_norm_mix[i], modrows, 0, 1, w_nat_qkv[j].astype(BF16), mp, ns, tm=tm_proj, tn=1024,
                          gains=gains, n_norm_cols=2 * d, head=NAT_HD)
            o_p = attn_block(y, 0, y, 1, y, 2, bp, seq, heads=NAT_HEADS, dq=NAT_HD, dv=NAT_HD)
            bias = _nat_bias(nat_rpb[j], ns // GRID_W)
            kc2d = cache_nat_k[:, j].reshape(nbs * past, d)
            vc2d = cache_nat_v[:, j].reshape(nbs * past, d)
            o_s = nat_latent(y, kc2d, vc2d, bias, mp, ns, nbs, past)
            w_o = w_nat_o[j]
            new_nat_k.append(y[:mp, d:2 * d].reshape(bp, seq, NAT_HEADS, NAT_HD))
            new_nat_v.append(y[:mp, 2 * d:].reshape(bp, seq, NAT_HEADS, NAT_HD))
        x = matmul_res(o_p, o_s, w_o.astype(BF16), x, modrows, 2, mp, ns, tm=tm)
        mlp = functools.partial(fused_mlp, x, g_norm_ffn[i], modrows, w_ff1_bf, w_ff2_bf, i, mp, ns, tm=tm, tf=1024)
        if i < depth - 1:
            x = mlp()
    return (mlp(row0=0, n_rows=mp).reshape(bp, seq, d), mlp(row0=mp, n_rows=ms).reshape(nbs, ns, d),
            jnp.stack(new_ckv, axis=1), jnp.stack(new_kpe, axis=1), jnp.stack(new_hg, axis=1),
            jnp.stack(new_nat_k, axis=1), jnp.stack(new_nat_v, axis=1))
```

```python
import functools
import math

import jax
import jax.numpy as jnp
import numpy as np
from jax import lax
from jax.experimental import pallas as pl
from jax.experimental.pallas import tpu as pltpu

F32 = jnp.float32
BF16 = jnp.bfloat16
EPS = 1e-6
ROPE_THETA = 10000.0
V7X_VMEM_LIMIT_BYTES = 56 * 1024 * 1024
LANES = 128
ROW_CHUNK = 256
ATTN_HEAD_GROUP = 4
ATTN_ROW_CHUNK = 256

N_MOD = 6
GRID_W = 64
MLA_HEADS, MLA_NOPE, MLA_ROPE, MLA_V = 16, 128, 64, 128
MLA_QK = MLA_NOPE + MLA_ROPE
MLA_QK_PAD = 256
MLA_Q_LORA, MLA_KV_LORA = 768, 512
HG_HEADS, HG_K, HG_V = 16, 128, 128
HG_TILE = 128
HG_UNROLL = 1
HG_MXU_LEVELS = 3
NAT_HEADS, NAT_HD = 16, 128
NAT_WIN_ROWS, NAT_WIN_COLS = 8, 16
NAT_HEADS_PER_STEP = 4
NAT_Q_ROWS = 4
NAT_K_ROWS = 12
NEG_BIG = -1e30
LOG2E = math.log2(math.e)


def _params(sem):
    return pltpu.CompilerParams(dimension_semantics=sem, vmem_limit_bytes=V7X_VMEM_LIMIT_BYTES)


def _mod_row(i, tm, mp, ns):
    start = i * tm
    return jnp.where(start < mp, 0, 1 + (start - mp) // ns)


def _sigmoid(x):
    return 0.5 * jnp.tanh(0.5 * x) + 0.5


def _norm_modulate(x, g, shift, scale):
    y = x * lax.rsqrt(jnp.mean(x * x, axis=-1, keepdims=True) + EPS) * g
    return y * (1.0 + scale) + shift


def _adaln_kernel(c_ref, w_ref, b_ref, o_ref):
    c = c_ref[...]
    s = (c * jax.nn.sigmoid(c)).astype(BF16)
    o_ref[0] = jnp.dot(s, w_ref[0].astype(BF16), preferred_element_type=F32) + b_ref[0]


def adaln_all(cvec, w_mod, b_mod):
    depth, d, n = w_mod.shape
    rows = cvec.shape[0]
    tn = 1024
    return pl.pallas_call(
        _adaln_kernel,
        grid=(depth, n // tn),
        in_specs=[
            pl.BlockSpec((rows, d), lambda l, j: (0, 0)),
            pl.BlockSpec((1, d, tn), lambda l, j: (l, 0, j)),
            pl.BlockSpec((1, 1, tn), lambda l, j: (l, 0, j)),
        ],
        out_specs=pl.BlockSpec((1, rows, tn), lambda l, j: (l, 0, j)),
        out_shape=jax.ShapeDtypeStruct((depth, rows, n), F32),
        compiler_params=_params(("parallel", "parallel")),
        name="adaln",
    )(cvec, w_mod, b_mod.reshape(depth, 1, n))


def _by_token_group(i, n_prompt_tiles, refs, body):
    if len(refs) == 1:
        body(refs[0])
    else:
        pl.when(i < n_prompt_tiles)(lambda: body(refs[0]))
        pl.when(i >= n_prompt_tiles)(lambda: body(refs[1]))


def _token_group_specs(block, n_prompt_tiles, n_src):
    if n_src == 1:
        return [pl.BlockSpec(block, lambda i, *_: (i, 0))]
    return [pl.BlockSpec(block, lambda i, *_: (jnp.minimum(i, n_prompt_tiles - 1), 0)),
            pl.BlockSpec(block, lambda i, *_: (jnp.maximum(i - n_prompt_tiles, 0), 0))]


def _nm_matmul_kernel(*refs, n_x, n_prompt_tiles, n_norm_tiles, head):
    x_refs = refs[:n_x]
    g_ref, sh_ref, sc_ref, w_ref, gain_ref, o_ref, h_scr = refs[n_x:]
    i, j = pl.program_id(0), pl.program_id(1)
    tm = h_scr.shape[0]

    def emit(rows, h, normed):
        acc = jnp.dot(h, w_ref[...], preferred_element_type=F32)
        if not normed:
            o_ref[rows, :] = acc
            return
        for c in range(acc.shape[1] // head):
            cols = slice(c * head, (c + 1) * head)
            blk = acc[:, cols]
            r = lax.rsqrt(jnp.mean(blk * blk, axis=-1, keepdims=True) + EPS)
            o_ref[rows, cols] = blk * r * gain_ref[:, cols]

    def first(x_ref):
        for r0 in range(0, tm, ROW_CHUNK):
            rows = slice(r0, r0 + ROW_CHUNK)
            h = _norm_modulate(x_ref[rows, :], g_ref[...], sh_ref[0], sc_ref[0]).astype(BF16)
            h_scr[rows, :] = h
            emit(rows, h, n_norm_tiles > 0)

    pl.when(j == 0)(lambda: _by_token_group(i, n_prompt_tiles, x_refs, first))
    if n_norm_tiles > 1:
        pl.when((j > 0) & (j < n_norm_tiles))(lambda: emit(slice(None), h_scr[...], True))
    pl.when(j >= max(n_norm_tiles, 1))(lambda: emit(slice(None), h_scr[...], False))


def nm_matmul(xs, g, modrows, k_shift, k_scale, w, mp, ns, *, tm, tn, gains=None, n_norm_cols=0, head=LANES):
    xs = xs if isinstance(xs, tuple) else (xs,)
    m, d = sum(x.shape[0] for x in xs), xs[0].shape[1]
    n = w.shape[1]
    if gains is None:
        gains = jnp.ones((1, n), F32)
    row = functools.partial(_mod_row, tm=tm, mp=mp, ns=ns)
    return pl.pallas_call(
        functools.partial(_nm_matmul_kernel, n_x=len(xs), n_prompt_tiles=mp // tm, n_norm_tiles=n_norm_cols // tn,
                          head=head),
        grid=(m // tm, n // tn),
        in_specs=_token_group_specs((tm, d), mp // tm, len(xs)) + [
            pl.BlockSpec((1, d), lambda i, j: (0, 0)),
            pl.BlockSpec((1, 1, d), lambda i, j: (row(i) * N_MOD + k_shift, 0, 0)),
            pl.BlockSpec((1, 1, d), lambda i, j: (row(i) * N_MOD + k_scale, 0, 0)),
            pl.BlockSpec((d, tn), lambda i, j: (0, j)),
            pl.BlockSpec((1, tn), lambda i, j: (0, j)),
        ],
        out_specs=pl.BlockSpec((tm, tn), lambda i, j: (i, j)),
        out_shape=jax.ShapeDtypeStruct((m, n), F32),
        scratch_shapes=[pltpu.VMEM((tm, d), BF16)],
        compiler_params=_params(("parallel", "arbitrary")),
        name="nm_matmul",
    )(*xs, g.reshape(1, d), modrows, modrows, w, gains)


def _matmul_res_kernel(*refs, n_x, n_prompt_tiles):
    ap_ref, as_ref, w_ref = refs[:3]
    x_refs = refs[3:3 + n_x]
    gt_ref, o_ref = refs[3 + n_x:]

    def finish(a_ref, x_ref):
        acc = jnp.dot(a_ref[...], w_ref[...], preferred_element_type=F32)
        o_ref[...] = x_ref[...] + gt_ref[0] * acc

    i = pl.program_id(0)
    pl.when(i < n_prompt_tiles)(lambda: finish(ap_ref, x_refs[0]))
    pl.when(i >= n_prompt_tiles)(lambda: finish(as_ref, x_refs[-1]))


def matmul_res(a_p, a_s, w, xs, modrows, k_gate, mp, ns, *, tm):
    xs = xs if isinstance(xs, tuple) else (xs,)
    kdim = a_p.shape[1]
    m = a_p.shape[0] + a_s.shape[0]
    n = w.shape[1]
    npt = mp // tm
    row = functools.partial(_mod_row, tm=tm, mp=mp, ns=ns)
    return pl.pallas_call(
        functools.partial(_matmul_res_kernel, n_x=len(xs), n_prompt_tiles=npt),
        grid=(m // tm,),
        in_specs=_token_group_specs((tm, kdim), npt, 2) + [pl.BlockSpec((kdim, n), lambda i: (0, 0))]
        + _token_group_specs((tm, n), npt, len(xs))
        + [pl.BlockSpec((1, 1, n), lambda i: (row(i) * N_MOD + k_gate, 0, 0))],
        out_specs=pl.BlockSpec((tm, n), lambda i: (i, 0)),
        out_shape=jax.ShapeDtypeStruct((m, n), F32),
        compiler_params=_params(("parallel",)),
        name="matmul_res",
    )(a_p, a_s, w, *xs, modrows)


def _mlp_kernel(x_ref, g_ref, sh_ref, sc_ref, gt_ref, w1_ref, w2_ref, o_ref, h_scr, acc_scr):
    f = pl.program_id(1)
    tm = x_ref.shape[0]

    def hidden(h):
        u = jnp.dot(h, w1_ref[...], preferred_element_type=F32)
        u = jnp.square(jnp.maximum(u, 0.0)).astype(BF16)
        return jnp.dot(u, w2_ref[...], preferred_element_type=F32)

    @pl.when(f == 0)
    def _():
        for r0 in range(0, tm, ROW_CHUNK):
            rows = slice(r0, r0 + ROW_CHUNK)
            h = _norm_modulate(x_ref[rows, :], g_ref[...], sh_ref[0], sc_ref[0]).astype(BF16)
            h_scr[rows, :] = h
            acc_scr[rows, :] = hidden(h)

    @pl.when(f > 0)
    def _():
        acc_scr[...] += hidden(h_scr[...])

    @pl.when(f == pl.num_programs(1) - 1)
    def _():
        o_ref[...] = x_ref[...] + gt_ref[0] * acc_scr[...]


def fused_mlp(x, g, modrows, w1, w2, layer, mp, ns, *, tm, tf, row0=0, n_rows=None):
    d = x.shape[1]
    m = x.shape[0] - row0 if n_rows is None else n_rows
    t0 = row0 // tm
    ff = w1.shape[2]
    row = functools.partial(_mod_row, tm=tm, mp=mp, ns=ns)
    modspec = lambda k: pl.BlockSpec((1, 1, d), lambda i, f: (row(i + t0) * N_MOD + k, 0, 0))
    return pl.pallas_call(
        _mlp_kernel,
        grid=(m // tm, ff // tf),
        in_specs=[
            pl.BlockSpec((tm, d), lambda i, f: (i + t0, 0)),
            pl.BlockSpec((1, d), lambda i, f: (0, 0)),
            modspec(3), modspec(4), modspec(5),
            pl.BlockSpec((None, d, tf), lambda i, f: (layer, 0, f)),
            pl.BlockSpec((None, tf, d), lambda i, f: (layer, f, 0)),
        ],
        out_specs=pl.BlockSpec((tm, d), lambda i, f: (i, 0)),
        out_shape=jax.ShapeDtypeStruct((m, d), F32),
        scratch_shapes=[pltpu.VMEM((tm, d), BF16), pltpu.VMEM((tm, d), F32)],
        compiler_params=_params(("parallel", "arbitrary")),
        name="fused_mlp",
    )(x, g.reshape(1, d), modrows, modrows, modrows, w1, w2)


def _rope_lanes(x, c_ref, s_ref):
    return x * c_ref[...] + pltpu.roll(x, LANES // 2, 1) * s_ref[...]


def _swap_tail(x):
    half = MLA_ROPE // 2
    pe = x[..., -MLA_ROPE:]
    return jnp.concatenate([x, pe[..., half:], pe[..., :half]], axis=-1)


def _mla_q_kernel(qa_ref, gqa_ref, w_ref, gqn_ref, c_ref, s_ref, o_ref):
    qa = qa_ref[...]
    qa = (qa * lax.rsqrt(jnp.mean(qa * qa, axis=-1, keepdims=True) + EPS) * gqa_ref[...]).astype(BF16)
    for h in range(MLA_HEADS):
        q = jnp.dot(qa, w_ref[:, h * MLA_QK_PAD:(h + 1) * MLA_QK_PAD], preferred_element_type=F32)
        nope, pe = q[:, :LANES], q[:, LANES:]
        ss = jnp.sum(nope * nope + 0.5 * (pe * pe), axis=-1, keepdims=True)
        r = lax.rsqrt(ss / MLA_QK + EPS)
        o_ref[:, h * MLA_QK_PAD:h * MLA_QK_PAD + LANES] = (nope * r * gqn_ref[:, :LANES]).astype(BF16)
        o_ref[:, h * MLA_QK_PAD + LANES:(h + 1) * MLA_QK_PAD] = _rope_lanes(
            pe * r * gqn_ref[:, LANES:], c_ref, s_ref).astype(BF16)


def _rope_spec(tm, mp, ns):
    def idx(i):
        start = i * tm
        return (jnp.where(start < mp, 0, 1 + ((start - mp) % ns) // tm), 0)

    return pl.BlockSpec((tm, LANES), idx)


def mla_q(y1, g_qa, w_qb_pad, g_qn_pad, tables, mp, ns, *, tm):
    m = y1.shape[0]
    n = MLA_HEADS * MLA_QK_PAD
    rs = _rope_spec(tm, mp, ns)
    return pl.pallas_call(
        _mla_q_kernel,
        grid=(m // tm,),
        in_specs=[
            pl.BlockSpec((tm, MLA_Q_LORA), lambda i: (i, 1)),
            pl.BlockSpec((1, MLA_Q_LORA), lambda i: (0, 0)),
            pl.BlockSpec((MLA_Q_LORA, n), lambda i: (0, 0)),
            pl.BlockSpec((1, MLA_QK_PAD), lambda i: (0, 0)),
            rs, rs,
        ],
        out_specs=pl.BlockSpec((tm, n), lambda i: (i, 0)),
        out_shape=jax.ShapeDtypeStruct((m, n), BF16),
        compiler_params=_params(("parallel",)),
        name="mla_q",
    )(y1, g_qa.reshape(1, MLA_Q_LORA), w_qb_pad, g_qn_pad, *tables)


def _mla_kv_kernel(ckv_ref, kpe_ref, gkva_ref, w_ref, gkn_ref, c_ref, s_ref,
                   ckv_out_ref, k_ref, v_ref, *, normalize):
    ckv = ckv_ref[...]
    if normalize:
        ckv = ckv * lax.rsqrt(jnp.mean(ckv * ckv, axis=-1, keepdims=True) + EPS) * gkva_ref[...]
    ckv_out_ref[...] = ckv
    ckv_b = ckv.astype(BF16)
    kpe = kpe_ref[...]
    pe_ss = 0.5 * jnp.sum(kpe * kpe, axis=-1, keepdims=True)
    pe_rot = _rope_lanes(kpe * gkn_ref[:, LANES:], c_ref, s_ref)
    for h in range(MLA_HEADS):
        kv = jnp.dot(ckv_b, w_ref[:, h * 2 * LANES:(h + 1) * 2 * LANES], preferred_element_type=F32)
        nope = kv[:, :LANES]
        r = lax.rsqrt((jnp.sum(nope * nope, axis=-1, keepdims=True) + pe_ss) / MLA_QK + EPS)
        k_ref[:, h * MLA_QK_PAD:h * MLA_QK_PAD + LANES] = (nope * r * gkn_ref[:, :LANES]).astype(BF16)
        k_ref[:, h * MLA_QK_PAD + LANES:(h + 1) * MLA_QK_PAD] = (pe_rot * r).astype(BF16)
        v_ref[:, h * MLA_V:(h + 1) * MLA_V] = kv[:, LANES:].astype(BF16)


def mla_kv(ckv_src, ckv_col, kpe_src, kpe_col, g_kva, w_kvb, g_kn_pad, tables, mp, ns, *, tm, normalize):
    m = ckv_src.shape[0]
    rs = _rope_spec(tm, mp, ns)
    return pl.pallas_call(
        functools.partial(_mla_kv_kernel, normalize=normalize),
        grid=(m // tm,),
        in_specs=[
            pl.BlockSpec((tm, MLA_KV_LORA), lambda i: (i, ckv_col)),
            pl.BlockSpec((tm, LANES), lambda i: (i, kpe_col)),
            pl.BlockSpec((1, MLA_KV_LORA), lambda i: (0, 0)),
            pl.BlockSpec((MLA_KV_LORA, MLA_HEADS * 2 * LANES), lambda i: (0, 0)),
            pl.BlockSpec((1, MLA_QK_PAD), lambda i: (0, 0)),
            rs, rs,
        ],
        out_specs=[
            pl.BlockSpec((tm, MLA_KV_LORA), lambda i: (i, 0)),
            pl.BlockSpec((tm, MLA_HEADS * MLA_QK_PAD), lambda i: (i, 0)),
            pl.BlockSpec((tm, MLA_HEADS * MLA_V), lambda i: (i, 0)),
        ],
        out_shape=[
            jax.ShapeDtypeStruct((m, MLA_KV_LORA), F32),
            jax.ShapeDtypeStruct((m, MLA_HEADS * MLA_QK_PAD), BF16),
            jax.ShapeDtypeStruct((m, MLA_HEADS * MLA_V), BF16),
        ],
        compiler_params=_params(("parallel",)),
        name="mla_kv",
    )(ckv_src, kpe_src, g_kva.reshape(1, MLA_KV_LORA), w_kvb, g_kn_pad, *tables)


def _rope_tables(tm, ns):
    n_freq = MLA_ROPE // 4
    inv = 1.0 / (ROPE_THETA ** (jnp.arange(n_freq, dtype=F32) / n_freq))
    t = jnp.arange(ns)
    row = (t // GRID_W).astype(F32)
    col = (t % GRID_W).astype(F32)
    ang = jnp.concatenate([row[:, None] * inv, col[:, None] * inv], axis=-1)
    cos, sin = jnp.cos(ang), jnp.sin(ang)
    z = jnp.zeros((ns, LANES - MLA_ROPE), F32)
    c = jnp.concatenate([cos, cos, z], axis=-1)
    s = jnp.concatenate([-sin, sin, z], axis=-1)
    ident = jnp.concatenate([jnp.ones((tm, MLA_ROPE), F32), jnp.zeros((tm, LANES - MLA_ROPE), F32)], axis=-1)
    return jnp.concatenate([ident, c], axis=0), jnp.concatenate([jnp.zeros((tm, LANES), F32), s], axis=0)


def _softmax_parts(scores):
    m = scores[0].max(axis=-1, keepdims=True)
    for s in scores[1:]:
        m = jnp.maximum(m, s.max(axis=-1, keepdims=True))
    ps = [jnp.exp2(s - m) for s in scores]
    l = ps[0].sum(axis=-1, keepdims=True)
    for p in ps[1:]:
        l = l + p.sum(axis=-1, keepdims=True)
    return ps, l


def _attend(jobs):
    ms = []
    for scores, _ in jobs:
        m = scores[0].max(axis=-1, keepdims=True)
        for s in scores[1:]:
            m = jnp.maximum(m, s.max(axis=-1, keepdims=True))
        ms.append(m)
    pss = [[jnp.exp2(s - m) for s in scores] for (scores, _), m in zip(jobs, ms)]
    ls = []
    for ps in pss:
        l = ps[0].sum(axis=-1, keepdims=True)
        for p in ps[1:]:
            l = l + p.sum(axis=-1, keepdims=True)
        ls.append(l)
    outs = []
    for (_, values), ps, l in zip(jobs, pss, ls):
        o = jnp.dot(ps[0].astype(BF16), values[0], preferred_element_type=F32)
        for p, v in zip(ps[1:], values[1:]):
            o = o + jnp.dot(p.astype(BF16), v, preferred_element_type=F32)
        outs.append(o / l)
    return outs


def _qkt(q, k):
    return lax.dot_general(q, k, (((1,), (1,)), ((), ())), preferred_element_type=F32)


def _attn_block_kernel(q_ref, k_ref, v_ref, o_ref, *, heads, dq, dv):
    for h0 in range(0, heads, ATTN_HEAD_GROUP):
        hs = range(h0, h0 + ATTN_HEAD_GROUP)
        jobs = [([_qkt(q_ref[:, h * dq:(h + 1) * dq].astype(BF16), k_ref[:, h * dq:(h + 1) * dq].astype(BF16))],
                 [v_ref[:, h * dv:(h + 1) * dv].astype(BF16)]) for h in hs]
        for h, o in zip(hs, _attend(jobs)):
            o_ref[:, h * dv:(h + 1) * dv] = o.astype(o_ref.dtype)


def attn_block(q_src, q_col, k_src, k_col, v_src, v_col, nb, seq, *, heads, dq, dv):
    return pl.pallas_call(
        functools.partial(_attn_block_kernel, heads=heads, dq=dq, dv=dv),
        grid=(nb,),
        in_specs=[
            pl.BlockSpec((seq, heads * dq), lambda b: (b, q_col)),
            pl.BlockSpec((seq, heads * dq), lambda b: (b, k_col)),
            pl.BlockSpec((seq, heads * dv), lambda b: (b, v_col)),
        ],
        out_specs=pl.BlockSpec((seq, heads * dv), lambda b: (b, 0)),
        out_shape=jax.ShapeDtypeStruct((nb * seq, heads * dv), BF16),
        compiler_params=_params(("parallel",)),
        name="attn_block",
    )(q_src, k_src, v_src)


def _attn_latent_kernel(q_ref, k_ref, v_ref, kc_ref, vc_ref, o_ref):
    chunks = [slice(r0, r0 + ATTN_ROW_CHUNK) for r0 in range(0, q_ref.shape[0], ATTN_ROW_CHUNK)]
    jobs = [([_qkt(q_ref[rows, :], k_ref[...]), _qkt(q_ref[rows, :], kc_ref[...])], [v_ref[...], vc_ref[...]])
            for rows in chunks]
    for rows, o in zip(chunks, _attend(jobs)):
        o_ref[rows, :] = o.astype(o_ref.dtype)


def attn_latent(q2d, k2d, v2d, kc2d, vc2d, mp, ns, nbs, past, *, heads, dq, dv, tq):
    return pl.pallas_call(
        _attn_latent_kernel,
        grid=(nbs, heads, ns // tq),
        in_specs=[
            pl.BlockSpec((tq, dq), lambda b, h, i: ((mp + b * ns) // tq + i, h)),
            pl.BlockSpec((ns, dq), lambda b, h, i: (mp // ns + b, h)),
            pl.BlockSpec((ns, dv), lambda b, h, i: (mp // ns + b, h)),
            pl.BlockSpec((past, dq), lambda b, h, i: (b, h)),
            pl.BlockSpec((past, dv), lambda b, h, i: (b, h)),
        ],
        out_specs=pl.BlockSpec((tq, dv), lambda b, h, i: (b * (ns // tq) + i, h)),
        out_shape=jax.ShapeDtypeStruct((nbs * ns, heads * dv), BF16),
        compiler_params=_params(("parallel", "parallel", "arbitrary")),
        name="attn_latent",
    )(q2d, k2d, v2d, kc2d, vc2d)


def _nat_key_row0(i, rows):
    return jnp.clip(NAT_Q_ROWS * i - NAT_WIN_ROWS // 2, 0, rows - NAT_K_ROWS)


def _nat_kernel(q_ref, k_ref, v_ref, kc_ref, vc_ref, b_ref, o_ref, *, rows):
    i = pl.program_id(2)
    start = pl.multiple_of(_nat_key_row0(i, rows) * GRID_W, GRID_W)
    win = pl.ds(start, NAT_K_ROWS * GRID_W)
    jobs = []
    for hd in range(NAT_HEADS_PER_STEP):
        cols = slice(hd * NAT_HD, (hd + 1) * NAT_HD)
        q = q_ref[:, cols].astype(BF16)
        s_win = _qkt(q, k_ref[win, cols].astype(BF16)) + b_ref[hd, 0]
        s_ctx = _qkt(q, kc_ref[:, cols].astype(BF16))
        jobs.append(([s_win, s_ctx], [v_ref[win, cols].astype(BF16), vc_ref[:, cols].astype(BF16)]))
    for hd, o in enumerate(_attend(jobs)):
        o_ref[:, hd * NAT_HD:(hd + 1) * NAT_HD] = o.astype(o_ref.dtype)


def _nat_bias_variants(rows):
    return (0, 1, rows // NAT_Q_ROWS - 1)


def _nat_bias(rpb, rows):
    w = GRID_W
    n_dr, n_dc = 2 * NAT_WIN_ROWS - 1, 2 * NAT_WIN_COLS - 1
    variants = _nat_bias_variants(rows)
    sel_r = np.zeros((len(variants), NAT_Q_ROWS, NAT_K_ROWS, n_dr), np.float32)
    for vi, i in enumerate(variants):
        row0 = min(max(NAT_Q_ROWS * i - NAT_WIN_ROWS // 2, 0), rows - NAT_K_ROWS)
        for a in range(NAT_Q_ROWS):
            qr = NAT_Q_ROWS * i + a
            r0 = min(max(qr - NAT_WIN_ROWS // 2, 0), rows - NAT_WIN_ROWS)
            for kl in range(NAT_K_ROWS):
                kr = row0 + kl
                if r0 <= kr < r0 + NAT_WIN_ROWS:
                    sel_r[vi, a, kl, kr - qr + NAT_WIN_ROWS - 1] = 1
    sel_c = np.zeros((n_dc, w, w), np.float32)
    for qc in range(w):
        c0 = min(max(qc - NAT_WIN_COLS // 2, 0), w - NAT_WIN_COLS)
        for kc in range(c0, c0 + NAT_WIN_COLS):
            sel_c[kc - qc + NAT_WIN_COLS - 1, qc, kc] = 1
    valid = (sel_r.sum(-1) > 0)[:, :, None, :, None] & (sel_c.sum(0) > 0)[None, None, :, None, :]
    rows_sel = jnp.einsum("hrd,vakr->hvakd", rpb.astype(F32), sel_r, precision=lax.Precision.HIGHEST)
    bias = jnp.einsum("hvakd,dqc->hvaqkc", rows_sel, sel_c, precision=lax.Precision.HIGHEST)
    bias = jnp.where(valid[None], bias * LOG2E, NEG_BIG)
    return bias.reshape(rpb.shape[0], len(variants), NAT_Q_ROWS * w, NAT_K_ROWS * w)


def nat_latent(y, kc2d, vc2d, bias, mp, ns, nbs, past):
    rows = ns // GRID_W
    tq = NAT_Q_ROWS * GRID_W
    nblk = ns // tq
    last = nblk - 1
    hps = NAT_HEADS_PER_STEP
    hg = NAT_HEADS // hps
    wide = hps * NAT_HD

    def bias_idx(b, h, i):
        return (h, jnp.where(i == 0, 0, jnp.where(i == last, 2, 1)), 0, 0)

    return pl.pallas_call(
        functools.partial(_nat_kernel, rows=rows),
        grid=(nbs, hg, nblk),
        in_specs=[
            pl.BlockSpec((tq, wide), lambda b, h, i: ((mp + b * ns) // tq + i, h)),
            pl.BlockSpec((ns, wide), lambda b, h, i: (mp // ns + b, hg + h)),
            pl.BlockSpec((ns, wide), lambda b, h, i: (mp // ns + b, 2 * hg + h)),
            pl.BlockSpec((past, wide), lambda b, h, i: (b, h)),
            pl.BlockSpec((past, wide), lambda b, h, i: (b, h)),
            pl.BlockSpec((hps, 1, tq, NAT_K_ROWS * GRID_W), bias_idx),
        ],
        out_specs=pl.BlockSpec((tq, wide), lambda b, h, i: (b * nblk + i, h)),
        out_shape=jax.ShapeDtypeStruct((nbs * ns, NAT_HEADS * NAT_HD), BF16),
        compiler_params=_params(("parallel", "parallel", "arbitrary")),
        name="nat_latent",
    )(y, y, y, kc2d, vc2d, bias)


def _hgrn_tables():
    t_ = HG_TILE
    nl = int(math.log2(t_))
    nm = HG_MXU_LEVELS
    a = np.zeros((2, (nm + 1) * t_, t_), np.float32)
    for lv in range(nm):
        w = 1 << lv
        for t in range(t_):
            mid = (t // (2 * w)) * 2 * w + w
            if t >= mid:
                a[0, lv * t_ + t, mid:t + 1] = 1
            else:
                a[0, lv * t_ + t, t + 1:mid] = 1
    for t in range(t_):
        a[0, nm * t_ + t, :t + 1] = 1
    for blk in range(nm + 1):
        a[1, blk * t_:(blk + 1) * t_] = a[0, blk * t_:(blk + 1) * t_][::-1, ::-1]
    lev = np.full((2, t_, t_), -1, np.int32)
    for t in range(t_):
        for s in range(t_):
            if s == t:
                lev[:, t, s] = nl
            elif s < t:
                lev[0, t, s] = (t ^ s).bit_length() - 1
            else:
                lev[1, t, s] = (t ^ s).bit_length() - 1
    return a, lev


def _hgrn_level_sums(cum, lv, d):
    w = 1 << lv
    parts = []
    for bs in range(0, HG_TILE, 2 * w):
        mid = bs + w
        brow = mid - 1 if d == 0 else mid
        b = cum[brow:brow + 1, :]
        lo, up = cum[bs:mid], cum[mid:bs + 2 * w]
        parts += [b - lo, up - b] if d == 0 else [lo - b, b - up]
    return jnp.concatenate(parts, axis=0)


def _hgrn_tiles(jobs, q_ref, f_refs, v_ref, lb_ref, a_ref, levs):
    t_ = HG_TILE
    nl = int(math.log2(t_))
    nm = HG_MXU_LEVELS
    acts = []
    for d, rows, cols, _ in jobs:
        lb = lb_ref[d:d + 1, cols]
        qr = q_ref[rows, cols]
        q = qr * _sigmoid(qr)
        f = lb + (1.0 - lb) * _sigmoid(f_refs[d][rows, cols])
        g = jnp.log(f)
        g_hi = g.astype(BF16)
        g_lo = (g - g_hi.astype(F32)).astype(BF16)
        acts.append((q, 1.0 - f, v_ref[rows, cols].astype(BF16), jnp.concatenate([g_hi, g_lo], axis=1)))
    sums = []
    for (d, _, _, _), (_, _, _, g2) in zip(jobs, acts):
        dd = jnp.dot(a_ref[d], g2, preferred_element_type=F32)
        sums.append(dd[:, :HG_K] + dd[:, HG_K:])
    atts = [jnp.where(levs[d] == nl, _qkt(q.astype(BF16), k.astype(BF16)), 0.0)
            for (d, _, _, _), (q, k, _, _) in zip(jobs, acts)]
    for lv in range(nl):
        for i, ((d, _, _, _), (q, k, _, _), dsum) in enumerate(zip(jobs, acts, sums)):
            cum = dsum[nm * t_:]
            e = jnp.exp(dsum[lv * t_:(lv + 1) * t_] if lv < nm else _hgrn_level_sums(cum, lv, d))
            atts[i] = jnp.where(levs[d] == lv, _qkt((q * e).astype(BF16), (k * e).astype(BF16)), atts[i])
    outs = []
    for (d, _, _, st), (q, k, v, _), dsum, att in zip(jobs, acts, sums, atts):
        cum = dsum[nm * t_:]
        o = jnp.dot(att.astype(BF16), v, preferred_element_type=F32)
        o = o + _qkt((q * jnp.exp(cum)).astype(BF16), st.astype(BF16))
        tot_row = t_ - 1 if d == 0 else 0
        total = cum[tot_row:tot_row + 1]
        k_out = (k * jnp.exp(total - cum)).astype(BF16)
        st = st * jnp.exp(total) + lax.dot_general(v, k_out, (((0,), (0,)), ((), ())), preferred_element_type=F32)
        outs.append((o, st))
    return outs


def _hgrn_kernel(q_ref, f0_ref, f1_ref, v_ref, gate_ref, lb_ref, go_ref, a_ref, lev_ref, s0_ref,
                 o_ref, st_ref, o_scr, *, n, has_state):
    t_ = HG_TILE
    nt = n // t_
    o_scr[...] = jnp.zeros_like(o_scr)
    levs = (lev_ref[0], lev_ref[1])
    hps = q_ref.shape[1] // HG_K
    chains = [(hd, d) for hd in range(hps) for d in range(2)]

    def tile_group(ti, sts):
        jobs = []
        for (hd, d), st in zip(chains, sts):
            tile = ti if d == 0 else nt - 1 - ti
            jobs.append((d, pl.ds(pl.multiple_of(tile * t_, t_), t_), slice(hd * HG_K, (hd + 1) * HG_K), st))
        outs = _hgrn_tiles(jobs, q_ref, (f0_ref, f1_ref), v_ref, lb_ref, a_ref, levs)
        for (_, rows, cols, _), (o, _) in zip(jobs, outs):
            o_scr[rows, cols] += o
        return tuple(st for _, st in outs)

    if has_state:
        st0 = tuple(s0_ref[0, d, hd].T for hd, d in chains)
    else:
        st0 = tuple(jnp.zeros((HG_V, HG_K), F32) for _ in chains)
    sts = lax.fori_loop(0, nt, tile_group, st0, unroll=HG_UNROLL)
    for (hd, d), st in zip(chains, sts):
        st_ref[0, d, hd] = st.T
    for hd in range(hps):
        cols = slice(hd * HG_V, (hd + 1) * HG_V)
        o = o_scr[:, cols]
        o = o * lax.rsqrt(jnp.mean(o * o, axis=-1, keepdims=True) + EPS) * go_ref[...]
        gate = gate_ref[:, cols]
        o_ref[:, cols] = (o * (gate * _sigmoid(gate))).astype(o_ref.dtype)


def hgrn_scan(y, lb, g_o, s0, row_blk0, nb, n, tables, *, hps):
    a, lev = tables
    hh = HG_HEADS // hps
    has_state = s0 is not None
    if s0 is None:
        s0 = jnp.zeros((1, 2, hps, HG_K, HG_V), F32)
        s0_spec = pl.BlockSpec((1, 2, hps, HG_K, HG_V), lambda b, h: (0, 0, 0, 0, 0))
    else:
        s0_spec = pl.BlockSpec((1, 2, hps, HG_K, HG_V), lambda b, h: (b, 0, h, 0, 0))
    col = lambda c: pl.BlockSpec((n, hps * HG_K), lambda b, h: (row_blk0 + b, c * hh + h))
    return pl.pallas_call(
        functools.partial(_hgrn_kernel, n=n, has_state=has_state),
        grid=(nb, hh),
        in_specs=[
            col(0), col(1), col(2), col(3), col(4),
            pl.BlockSpec((2, hps * HG_K), lambda b, h: (0, h)),
            pl.BlockSpec((1, HG_V), lambda b, h: (0, 0)),
            pl.BlockSpec(a.shape, lambda b, h: (0, 0, 0)),
            pl.BlockSpec(lev.shape, lambda b, h: (0, 0, 0)),
            s0_spec,
        ],
        out_specs=[
            pl.BlockSpec((n, hps * HG_V), lambda b, h: (b, h)),
            pl.BlockSpec((1, 2, hps, HG_K, HG_V), lambda b, h: (b, 0, h, 0, 0)),
        ],
        out_shape=[
            jax.ShapeDtypeStruct((nb * n, HG_HEADS * HG_V), BF16),
            jax.ShapeDtypeStruct((nb, 2, HG_HEADS, HG_K, HG_V), F32),
        ],
        scratch_shapes=[pltpu.VMEM((n, hps * HG_V), F32)],
        compiler_params=_params(("parallel", "arbitrary")),
        name="hgrn_scan",
    )(y, y, y, y, y, lb, g_o.reshape(1, HG_V), a, lev, s0)


def kernel(x_prompt, x_sample, cache_mla_ckv, cache_mla_kpe, state_hgrn, cache_nat_k, cache_nat_v, c, c_ctx, w_mod, b_mod, g_norm_mix, g_norm_ffn, w_mla_qa, g_mla_qa, w_mla_qb, g_mla_qn, w_mla_kva, g_mla_kva, w_mla_kvb, g_mla_kn, w_mla_o, w_hg_q, w_hg_f, hg_lower_bounds, w_hg_i, w_hg_g, g_hg_o, w_hg_o, w_nat_qkv, g_nat_q, g_nat_k, nat_rpb, w_nat_o, w_ff1, w_ff2):
    bp, seq, d = x_prompt.shape
    nbs, ns, _ = x_sample.shape
    past = cache_mla_ckv.shape[2]
    depth = w_mod.shape[0]
    mp, ms = bp * seq, nbs * ns
    tm = 512
    tm_proj = 1024

    x = (x_prompt.reshape(mp, d), x_sample.reshape(ms, d))
    cvec = jnp.zeros((8, d), F32).at[0].set(c_ctx).at[1:1 + nbs].set(c)
    mods = adaln_all(cvec, w_mod, b_mod)

    lb_all = jnp.cumsum(jax.nn.softmax(hg_lower_bounds.astype(F32), axis=1), axis=1)
    lb_all = lb_all - lb_all[:, :1]
    rope_tabs = _rope_tables(tm, ns)
    hg_tabs = _hgrn_tables()
    hg_tabs = (jnp.asarray(hg_tabs[0], BF16), jnp.asarray(hg_tabs[1]))

    w_ff1_bf, w_ff2_bf = w_ff1.astype(BF16), w_ff2.astype(BF16)

    new_ckv, new_kpe, new_hg, new_nat_k, new_nat_v = [], [], [], [], []
    for i in range(depth):
        kind, j = i % 3, i // 3
        modrows = mods[i].reshape(8 * N_MOD, 1, d)
        if kind == 0:
            w1 = jnp.concatenate(
                [_swap_tail(w_mla_kva[j]), jnp.zeros((d, MLA_Q_LORA - MLA_KV_LORA - 2 * MLA_ROPE), F32), w_mla_qa[j]],
                axis=1).astype(BF16)
            y1 = nm_matmul(x, g_norm_mix[i], modrows, 0, 1, w1, mp, ns, tm=tm if isinstance(x, tuple) else tm_proj,
                           tn=MLA_Q_LORA)
            w_qb_pad = _swap_tail(w_mla_qb[j].reshape(MLA_Q_LORA, MLA_HEADS, MLA_QK)).reshape(
                MLA_Q_LORA, MLA_HEADS * MLA_QK_PAD).astype(BF16)
            g_qn_pad = _swap_tail(g_mla_qn[j].reshape(1, MLA_QK)) * (MLA_QK ** -0.5 * LOG2E)
            g_kn_pad = _swap_tail(g_mla_kn[j].reshape(1, MLA_QK))
            w_kvb = w_mla_kvb[j].astype(BF16)
            q2d = mla_q(y1, g_mla_qa[j], w_qb_pad, g_qn_pad, rope_tabs, mp, ns, tm=tm)
            ckv, k2d, v2d = mla_kv(y1, 0, y1, MLA_KV_LORA // LANES, g_mla_kva[j], w_kvb, g_kn_pad, rope_tabs,
                                   mp, ns, tm=tm, normalize=True)
            ckv_c = cache_mla_ckv[:, j].reshape(nbs * past, MLA_KV_LORA)
            kpe_c = _swap_tail(cache_mla_kpe[:, j].reshape(nbs * past, MLA_ROPE))
            ident_tabs = tuple(t[:past] for t in _rope_tables(past, ns))
            _, kc2d, vc2d = mla_kv(ckv_c, 0, kpe_c, 0, g_mla_kva[j], w_kvb, g_kn_pad, ident_tabs,
                                   nbs * past, ns, tm=past, normalize=False)
            o_p = attn_block(q2d, 0, k2d, 0, v2d, 0, bp, seq, heads=MLA_HEADS, dq=MLA_QK_PAD, dv=MLA_V)
            o_s = attn_latent(q2d, k2d, v2d, kc2d, vc2d, mp, ns, nbs, past, heads=MLA_HEADS, dq=MLA_QK_PAD,
                              dv=MLA_V, tq=1024)
            w_o = w_mla_o[j]
            new_ckv.append(ckv[:mp].reshape(bp, seq, MLA_KV_LORA))
            new_kpe.append(y1[:mp, MLA_KV_LORA:MLA_KV_LORA + MLA_ROPE].reshape(bp, seq, MLA_ROPE))
        elif kind == 1:
            w_all = jnp.concatenate([w_hg_q[j], w_hg_f[j, 0], w_hg_f[j, 1], w_hg_i[j], w_hg_g[j]], axis=1).astype(BF16)
            y = nm_matmul(x, g_norm_mix[i], modrows, 0, 1, w_all, mp, ns, tm=tm_proj, tn=1024)
            lb = lb_all[:, i]
            o_p, s_ctx = hgrn_scan(y, lb, g_hg_o[j], None, 0, bp, seq, hg_tabs, hps=4)
            o_s, _ = hgrn_scan(y, lb, g_hg_o[j], state_hgrn[:, j], mp // ns, nbs, ns, hg_tabs, hps=2)
            w_o = w_hg_o[j]
            new_hg.append(s_ctx)
        else:
            gains = jnp.concatenate([jnp.tile(g_nat_q[j] * (NAT_HD ** -0.5 * LOG2E), NAT_HEADS), jnp.tile(g_nat_k[j], NAT_HEADS),
                                     jnp.ones((d,), F32)]).reshape(1, 3 * d)
            y = nm_matmul(x, g_norm_mix[i], modrows, 0, 1, w_nat_qkv[j].astype(BF16), mp, ns, tm=tm_proj, tn=1024,
                          gains=gains, n_norm_cols=2 * d, head=NAT_HD)
            o_p = attn_block(y, 0, y, 1, y, 2, bp, seq, heads=NAT_HEADS, dq=NAT_HD, dv=NAT_HD)
            bias = _nat_bias(nat_rpb[j], ns // GRID_W)
            kc2d = cache_nat_k[:, j].reshape(nbs * past, d)
            vc2d = cache_nat_v[:, j].reshape(nbs * past, d)
            o_s = nat_latent(y, kc2d, vc2d, bias, mp, ns, nbs, past)
            w_o = w_nat_o[j]
            new_nat_k.append(y[:mp, d:2 * d].reshape(bp, seq, NAT_HEADS, NAT_HD))
            new_nat_v.append(y[:mp, 2 * d:].reshape(bp, seq, NAT_HEADS, NAT_HD))
        x = matmul_res(o_p, o_s, w_o.astype(BF16), x, modrows, 2, mp, ns, tm=tm)
        mlp = functools.partial(fused_mlp, x, g_norm_ffn[i], modrows, w_ff1_bf, w_ff2_bf, i, mp, ns, tm=tm, tf=1024)
        if i < depth - 1:
            x = mlp()
    return (mlp(row0=0, n_rows=mp).reshape(bp, seq, d), mlp(row0=mp, n_rows=ms).reshape(nbs, ns, d),
            jnp.stack(new_ckv, axis=1), jnp.stack(new_kpe, axis=1), jnp.stack(new_hg, axis=1),
            jnp.stack(new_nat_k, axis=1), jnp.stack(new_nat_v, axis=1))
```

```python
import functools
import math

import jax
import jax.numpy as jnp
import numpy as np
from jax import lax
from jax.experimental import pallas as pl
from jax.experimental.pallas import tpu as pltpu

F32 = jnp.float32
BF16 = jnp.bfloat16
EPS = 1e-6
ROPE_THETA = 10000.0
V7X_VMEM_LIMIT_BYTES = 56 * 1024 * 1024
LANES = 128
ROW_CHUNK = 256
ATTN_HEAD_GROUP = 4
ATTN_ROW_CHUNK = 256

N_MOD = 6
GRID_W = 64
MLA_HEADS, MLA_NOPE, MLA_ROPE, MLA_V = 16, 128, 64, 128
MLA_QK = MLA_NOPE + MLA_ROPE
MLA_QK_PAD = 256
MLA_Q_LORA, MLA_KV_LORA = 768, 512
HG_HEADS, HG_K, HG_V = 16, 128, 128
HG_TILE = 128
HG_FLAT_TILES = 2
HG_UNROLL = 1
HG_MXU_LEVELS = 3
NAT_HEADS, NAT_HD = 16, 128
NAT_WIN_ROWS, NAT_WIN_COLS = 8, 16
NAT_HEADS_PER_STEP = 4
NAT_Q_ROWS = 4
NAT_K_ROWS = 12
NEG_BIG = -1e30
LOG2E = math.log2(math.e)


def _params(sem):
    return pltpu.CompilerParams(dimension_semantics=sem, vmem_limit_bytes=V7X_VMEM_LIMIT_BYTES)


def _mod_row(i, tm, mp, ns):
    start = i * tm
    return jnp.where(start < mp, 0, 1 + (start - mp) // ns)


def _sigmoid(x):
    return 0.5 * jnp.tanh(0.5 * x) + 0.5


def _norm_modulate(x, g, shift, scale):
    y = x * lax.rsqrt(jnp.mean(x * x, axis=-1, keepdims=True) + EPS) * g
    return y * (1.0 + scale) + shift


def _adaln_kernel(c_ref, w_ref, b_ref, o_ref):
    c = c_ref[...]
    s = (c * jax.nn.sigmoid(c)).astype(BF16)
    o_ref[0] = jnp.dot(s, w_ref[0].astype(BF16), preferred_element_type=F32) + b_ref[0]


def adaln_all(cvec, w_mod, b_mod):
    depth, d, n = w_mod.shape
    rows = cvec.shape[0]
    tn = 1024
    return pl.pallas_call(
        _adaln_kernel,
        grid=(depth, n // tn),
        in_specs=[
            pl.BlockSpec((rows, d), lambda l, j: (0, 0)),
            pl.BlockSpec((1, d, tn), lambda l, j: (l, 0, j)),
            pl.BlockSpec((1, 1, tn), lambda l, j: (l, 0, j)),
        ],
        out_specs=pl.BlockSpec((1, rows, tn), lambda l, j: (l, 0, j)),
        out_shape=jax.ShapeDtypeStruct((depth, rows, n), F32),
        compiler_params=_params(("parallel", "parallel")),
        name="adaln",
    )(cvec, w_mod, b_mod.reshape(depth, 1, n))


def _by_token_group(i, n_prompt_tiles, refs, body):
    if len(refs) == 1:
        body(refs[0])
    else:
        pl.when(i < n_prompt_tiles)(lambda: body(refs[0]))
        pl.when(i >= n_prompt_tiles)(lambda: body(refs[1]))


def _token_group_specs(block, n_prompt_tiles, n_src):
    if n_src == 1:
        return [pl.BlockSpec(block, lambda i, *_: (i, 0))]
    return [pl.BlockSpec(block, lambda i, *_: (jnp.minimum(i, n_prompt_tiles - 1), 0)),
            pl.BlockSpec(block, lambda i, *_: (jnp.maximum(i - n_prompt_tiles, 0), 0))]


def _nm_matmul_kernel(*refs, n_x, n_prompt_tiles, n_norm_tiles, head):
    x_refs = refs[:n_x]
    g_ref, sh_ref, sc_ref, w_ref, gain_ref, o_ref, h_scr = refs[n_x:]
    i, j = pl.program_id(0), pl.program_id(1)
    tm = h_scr.shape[0]

    def emit(rows, h, normed):
        acc = jnp.dot(h, w_ref[...], preferred_element_type=F32)
        if not normed:
            o_ref[rows, :] = acc
            return
        for c in range(acc.shape[1] // head):
            cols = slice(c * head, (c + 1) * head)
            blk = acc[:, cols]
            r = lax.rsqrt(jnp.mean(blk * blk, axis=-1, keepdims=True) + EPS)
            o_ref[rows, cols] = blk * r * gain_ref[:, cols]

    def first(x_ref):
        for r0 in range(0, tm, ROW_CHUNK):
            rows = slice(r0, r0 + ROW_CHUNK)
            h = _norm_modulate(x_ref[rows, :], g_ref[...], sh_ref[0], sc_ref[0]).astype(BF16)
            h_scr[rows, :] = h
            emit(rows, h, n_norm_tiles > 0)

    pl.when(j == 0)(lambda: _by_token_group(i, n_prompt_tiles, x_refs, first))
    if n_norm_tiles > 1:
        pl.when((j > 0) & (j < n_norm_tiles))(lambda: emit(slice(None), h_scr[...], True))
    pl.when(j >= max(n_norm_tiles, 1))(lambda: emit(slice(None), h_scr[...], False))


def nm_matmul(xs, g, modrows, k_shift, k_scale, w, mp, ns, *, tm, tn, gains=None, n_norm_cols=0, head=LANES):
    xs = xs if isinstance(xs, tuple) else (xs,)
    m, d = sum(x.shape[0] for x in xs), xs[0].shape[1]
    n = w.shape[1]
    if gains is None:
        gains = jnp.ones((1, n), F32)
    row = functools.partial(_mod_row, tm=tm, mp=mp, ns=ns)
    return pl.pallas_call(
        functools.partial(_nm_matmul_kernel, n_x=len(xs), n_prompt_tiles=mp // tm, n_norm_tiles=n_norm_cols // tn,
                          head=head),
        grid=(m // tm, n // tn),
        in_specs=_token_group_specs((tm, d), mp // tm, len(xs)) + [
            pl.BlockSpec((1, d), lambda i, j: (0, 0)),
            pl.BlockSpec((1, 1, d), lambda i, j: (row(i) * N_MOD + k_shift, 0, 0)),
            pl.BlockSpec((1, 1, d), lambda i, j: (row(i) * N_MOD + k_scale, 0, 0)),
            pl.BlockSpec((d, tn), lambda i, j: (0, j)),
            pl.BlockSpec((1, tn), lambda i, j: (0, j)),
        ],
        out_specs=pl.BlockSpec((tm, tn), lambda i, j: (i, j)),
        out_shape=jax.ShapeDtypeStruct((m, n), F32),
        scratch_shapes=[pltpu.VMEM((tm, d), BF16)],
        compiler_params=_params(("parallel", "arbitrary")),
        name="nm_matmul",
    )(*xs, g.reshape(1, d), modrows, modrows, w, gains)


def _matmul_res_kernel(*refs, n_x, n_prompt_tiles):
    ap_ref, as_ref, w_ref = refs[:3]
    x_refs = refs[3:3 + n_x]
    gt_ref, o_ref = refs[3 + n_x:]

    def finish(a_ref, x_ref):
        acc = jnp.dot(a_ref[...], w_ref[...], preferred_element_type=F32)
        o_ref[...] = x_ref[...] + gt_ref[0] * acc

    i = pl.program_id(0)
    pl.when(i < n_prompt_tiles)(lambda: finish(ap_ref, x_refs[0]))
    pl.when(i >= n_prompt_tiles)(lambda: finish(as_ref, x_refs[-1]))


def matmul_res(a_p, a_s, w, xs, modrows, k_gate, mp, ns, *, tm):
    xs = xs if isinstance(xs, tuple) else (xs,)
    kdim = a_p.shape[1]
    m = a_p.shape[0] + a_s.shape[0]
    n = w.shape[1]
    npt = mp // tm
    row = functools.partial(_mod_row, tm=tm, mp=mp, ns=ns)
    return pl.pallas_call(
        functools.partial(_matmul_res_kernel, n_x=len(xs), n_prompt_tiles=npt),
        grid=(m // tm,),
        in_specs=_token_group_specs((tm, kdim), npt, 2) + [pl.BlockSpec((kdim, n), lambda i: (0, 0))]
        + _token_group_specs((tm, n), npt, len(xs))
        + [pl.BlockSpec((1, 1, n), lambda i: (row(i) * N_MOD + k_gate, 0, 0))],
        out_specs=pl.BlockSpec((tm, n), lambda i: (i, 0)),
        out_shape=jax.ShapeDtypeStruct((m, n), F32),
        compiler_params=_params(("parallel",)),
        name="matmul_res",
    )(a_p, a_s, w, *xs, modrows)


def _mlp_kernel(x_ref, g_ref, sh_ref, sc_ref, gt_ref, w1_ref, w2_ref, o_ref, h_scr, acc_scr):
    f = pl.program_id(1)
    tm = x_ref.shape[0]

    def hidden(h):
        u = jnp.dot(h, w1_ref[...], preferred_element_type=F32)
        u = jnp.square(jnp.maximum(u, 0.0)).astype(BF16)
        return jnp.dot(u, w2_ref[...], preferred_element_type=F32)

    @pl.when(f == 0)
    def _():
        for r0 in range(0, tm, ROW_CHUNK):
            rows = slice(r0, r0 + ROW_CHUNK)
            h = _norm_modulate(x_ref[rows, :], g_ref[...], sh_ref[0], sc_ref[0]).astype(BF16)
            h_scr[rows, :] = h
            acc_scr[rows, :] = hidden(h)

    @pl.when(f > 0)
    def _():
        acc_scr[...] += hidden(h_scr[...])

    @pl.when(f == pl.num_programs(1) - 1)
    def _():
        o_ref[...] = x_ref[...] + gt_ref[0] * acc_scr[...]


def fused_mlp(x, g, modrows, w1, w2, layer, mp, ns, *, tm, tf, row0=0, n_rows=None):
    d = x.shape[1]
    m = x.shape[0] - row0 if n_rows is None else n_rows
    t0 = row0 // tm
    ff = w1.shape[2]
    row = functools.partial(_mod_row, tm=tm, mp=mp, ns=ns)
    modspec = lambda k: pl.BlockSpec((1, 1, d), lambda i, f: (row(i + t0) * N_MOD + k, 0, 0))
    return pl.pallas_call(
        _mlp_kernel,
        grid=(m // tm, ff // tf),
        in_specs=[
            pl.BlockSpec((tm, d), lambda i, f: (i + t0, 0)),
            pl.BlockSpec((1, d), lambda i, f: (0, 0)),
            modspec(3), modspec(4), modspec(5),
            pl.BlockSpec((None, d, tf), lambda i, f: (layer, 0, f)),
            pl.BlockSpec((None, tf, d), lambda i, f: (layer, f, 0)),
        ],
        out_specs=pl.BlockSpec((tm, d), lambda i, f: (i, 0)),
        out_shape=jax.ShapeDtypeStruct((m, d), F32),
        scratch_shapes=[pltpu.VMEM((tm, d), BF16), pltpu.VMEM((tm, d), F32)],
        compiler_params=_params(("parallel", "arbitrary")),
        name="fused_mlp",
    )(x, g.reshape(1, d), modrows, modrows, modrows, w1, w2)


def _rope_lanes(x, c_ref, s_ref):
    return x * c_ref[...] + pltpu.roll(x, LANES // 2, 1) * s_ref[...]


def _swap_tail(x):
    half = MLA_ROPE // 2
    pe = x[..., -MLA_ROPE:]
    return jnp.concatenate([x, pe[..., half:], pe[..., :half]], axis=-1)


def _mla_q_kernel(qa_ref, gqa_ref, w_ref, gqn_ref, c_ref, s_ref, o_ref):
    qa = qa_ref[...]
    qa = (qa * lax.rsqrt(jnp.mean(qa * qa, axis=-1, keepdims=True) + EPS) * gqa_ref[...]).astype(BF16)
    for h in range(MLA_HEADS):
        q = jnp.dot(qa, w_ref[:, h * MLA_QK_PAD:(h + 1) * MLA_QK_PAD], preferred_element_type=F32)
        nope, pe = q[:, :LANES], q[:, LANES:]
        ss = jnp.sum(nope * nope + 0.5 * (pe * pe), axis=-1, keepdims=True)
        r = lax.rsqrt(ss / MLA_QK + EPS)
        o_ref[:, h * MLA_QK_PAD:h * MLA_QK_PAD + LANES] = (nope * r * gqn_ref[:, :LANES]).astype(BF16)
        o_ref[:, h * MLA_QK_PAD + LANES:(h + 1) * MLA_QK_PAD] = _rope_lanes(
            pe * r * gqn_ref[:, LANES:], c_ref, s_ref).astype(BF16)


def _rope_spec(tm, mp, ns):
    def idx(i):
        start = i * tm
        return (jnp.where(start < mp, 0, 1 + ((start - mp) % ns) // tm), 0)

    return pl.BlockSpec((tm, LANES), idx)


def mla_q(y1, g_qa, w_qb_pad, g_qn_pad, tables, mp, ns, *, tm):
    m = y1.shape[0]
    n = MLA_HEADS * MLA_QK_PAD
    rs = _rope_spec(tm, mp, ns)
    return pl.pallas_call(
        _mla_q_kernel,
        grid=(m // tm,),
        in_specs=[
            pl.BlockSpec((tm, MLA_Q_LORA), lambda i: (i, 1)),
            pl.BlockSpec((1, MLA_Q_LORA), lambda i: (0, 0)),
            pl.BlockSpec((MLA_Q_LORA, n), lambda i: (0, 0)),
            pl.BlockSpec((1, MLA_QK_PAD), lambda i: (0, 0)),
            rs, rs,
        ],
        out_specs=pl.BlockSpec((tm, n), lambda i: (i, 0)),
        out_shape=jax.ShapeDtypeStruct((m, n), BF16),
        compiler_params=_params(("parallel",)),
        name="mla_q",
    )(y1, g_qa.reshape(1, MLA_Q_LORA), w_qb_pad, g_qn_pad, *tables)


def _mla_kv_kernel(ckv_ref, kpe_ref, gkva_ref, w_ref, gkn_ref, c_ref, s_ref,
                   ckv_out_ref, k_ref, v_ref, *, normalize):
    ckv = ckv_ref[...]
    if normalize:
        ckv = ckv * lax.rsqrt(jnp.mean(ckv * ckv, axis=-1, keepdims=True) + EPS) * gkva_ref[...]
    ckv_out_ref[...] = ckv
    ckv_b = ckv.astype(BF16)
    kpe = kpe_ref[...]
    pe_ss = 0.5 * jnp.sum(kpe * kpe, axis=-1, keepdims=True)
    pe_rot = _rope_lanes(kpe * gkn_ref[:, LANES:], c_ref, s_ref)
    for h in range(MLA_HEADS):
        kv = jnp.dot(ckv_b, w_ref[:, h * 2 * LANES:(h + 1) * 2 * LANES], preferred_element_type=F32)
        nope = kv[:, :LANES]
        r = lax.rsqrt((jnp.sum(nope * nope, axis=-1, keepdims=True) + pe_ss) / MLA_QK + EPS)
        k_ref[:, h * MLA_QK_PAD:h * MLA_QK_PAD + LANES] = (nope * r * gkn_ref[:, :LANES]).astype(BF16)
        k_ref[:, h * MLA_QK_PAD + LANES:(h + 1) * MLA_QK_PAD] = (pe_rot * r).astype(BF16)
        v_ref[:, h * MLA_V:(h + 1) * MLA_V] = kv[:, LANES:].astype(BF16)


def mla_kv(ckv_src, ckv_col, kpe_src, kpe_col, g_kva, w_kvb, g_kn_pad, tables, mp, ns, *, tm, normalize):
    m = ckv_src.shape[0]
    rs = _rope_spec(tm, mp, ns)
    return pl.pallas_call(
        functools.partial(_mla_kv_kernel, normalize=normalize),
        grid=(m // tm,),
        in_specs=[
            pl.BlockSpec((tm, MLA_KV_LORA), lambda i: (i, ckv_col)),
            pl.BlockSpec((tm, LANES), lambda i: (i, kpe_col)),
            pl.BlockSpec((1, MLA_KV_LORA), lambda i: (0, 0)),
            pl.BlockSpec((MLA_KV_LORA, MLA_HEADS * 2 * LANES), lambda i: (0, 0)),
            pl.BlockSpec((1, MLA_QK_PAD), lambda i: (0, 0)),
            rs, rs,
        ],
        out_specs=[
            pl.BlockSpec((tm, MLA_KV_LORA), lambda i: (i, 0)),
            pl.BlockSpec((tm, MLA_HEADS * MLA_QK_PAD), lambda i: (i, 0)),
            pl.BlockSpec((tm, MLA_HEADS * MLA_V), lambda i: (i, 0)),
        ],
        out_shape=[
            jax.ShapeDtypeStruct((m, MLA_KV_LORA), F32),
            jax.ShapeDtypeStruct((m, MLA_HEADS * MLA_QK_PAD), BF16),
            jax.ShapeDtypeStruct((m, MLA_HEADS * MLA_V), BF16),
        ],
        compiler_params=_params(("parallel",)),
        name="mla_kv",
    )(ckv_src, kpe_src, g_kva.reshape(1, MLA_KV_LORA), w_kvb, g_kn_pad, *tables)


def _rope_tables(tm, ns):
    n_freq = MLA_ROPE // 4
    inv = 1.0 / (ROPE_THETA ** (jnp.arange(n_freq, dtype=F32) / n_freq))
    t = jnp.arange(ns)
    row = (t // GRID_W).astype(F32)
    col = (t % GRID_W).astype(F32)
    ang = jnp.concatenate([row[:, None] * inv, col[:, None] * inv], axis=-1)
    cos, sin = jnp.cos(ang), jnp.sin(ang)
    z = jnp.zeros((ns, LANES - MLA_ROPE), F32)
    c = jnp.concatenate([cos, cos, z], axis=-1)
    s = jnp.concatenate([-sin, sin, z], axis=-1)
    ident = jnp.concatenate([jnp.ones((tm, MLA_ROPE), F32), jnp.zeros((tm, LANES - MLA_ROPE), F32)], axis=-1)
    return jnp.concatenate([ident, c], axis=0), jnp.concatenate([jnp.zeros((tm, LANES), F32), s], axis=0)


def _softmax_parts(scores):
    m = scores[0].max(axis=-1, keepdims=True)
    for s in scores[1:]:
        m = jnp.maximum(m, s.max(axis=-1, keepdims=True))
    ps = [jnp.exp2(s - m) for s in scores]
    l = ps[0].sum(axis=-1, keepdims=True)
    for p in ps[1:]:
        l = l + p.sum(axis=-1, keepdims=True)
    return ps, l


def _attend(jobs):
    ms = []
    for scores, _ in jobs:
        m = scores[0].max(axis=-1, keepdims=True)
        for s in scores[1:]:
            m = jnp.maximum(m, s.max(axis=-1, keepdims=True))
        ms.append(m)
    pss = [[jnp.exp2(s - m) for s in scores] for (scores, _), m in zip(jobs, ms)]
    ls = []
    for ps in pss:
        l = ps[0].sum(axis=-1, keepdims=True)
        for p in ps[1:]:
            l = l + p.sum(axis=-1, keepdims=True)
        ls.append(l)
    outs = []
    for (_, values), ps, l in zip(jobs, pss, ls):
        o = jnp.dot(ps[0].astype(BF16), values[0], preferred_element_type=F32)
        for p, v in zip(ps[1:], values[1:]):
            o = o + jnp.dot(p.astype(BF16), v, preferred_element_type=F32)
        outs.append(o / l)
    return outs


def _qkt(q, k):
    return lax.dot_general(q, k, (((1,), (1,)), ((), ())), preferred_element_type=F32)


def _attn_block_kernel(q_ref, k_ref, v_ref, o_ref, *, heads, dq, dv):
    for h0 in range(0, heads, ATTN_HEAD_GROUP):
        hs = range(h0, h0 + ATTN_HEAD_GROUP)
        jobs = [([_qkt(q_ref[:, h * dq:(h + 1) * dq].astype(BF16), k_ref[:, h * dq:(h + 1) * dq].astype(BF16))],
                 [v_ref[:, h * dv:(h + 1) * dv].astype(BF16)]) for h in hs]
        for h, o in zip(hs, _attend(jobs)):
            o_ref[:, h * dv:(h + 1) * dv] = o.astype(o_ref.dtype)


def attn_block(q_src, q_col, k_src, k_col, v_src, v_col, nb, seq, *, heads, dq, dv):
    return pl.pallas_call(
        functools.partial(_attn_block_kernel, heads=heads, dq=dq, dv=dv),
        grid=(nb,),
        in_specs=[
            pl.BlockSpec((seq, heads * dq), lambda b: (b, q_col)),
            pl.BlockSpec((seq, heads * dq), lambda b: (b, k_col)),
            pl.BlockSpec((seq, heads * dv), lambda b: (b, v_col)),
        ],
        out_specs=pl.BlockSpec((seq, heads * dv), lambda b: (b, 0)),
        out_shape=jax.ShapeDtypeStruct((nb * seq, heads * dv), BF16),
        compiler_params=_params(("parallel",)),
        name="attn_block",
    )(q_src, k_src, v_src)


def _attn_latent_kernel(q_ref, k_ref, v_ref, kc_ref, vc_ref, o_ref):
    chunks = [slice(r0, r0 + ATTN_ROW_CHUNK) for r0 in range(0, q_ref.shape[0], ATTN_ROW_CHUNK)]
    jobs = [([_qkt(q_ref[rows, :], k_ref[...]), _qkt(q_ref[rows, :], kc_ref[...])], [v_ref[...], vc_ref[...]])
            for rows in chunks]
    for rows, o in zip(chunks, _attend(jobs)):
        o_ref[rows, :] = o.astype(o_ref.dtype)


def attn_latent(q2d, k2d, v2d, kc2d, vc2d, mp, ns, nbs, past, *, heads, dq, dv, tq):
    return pl.pallas_call(
        _attn_latent_kernel,
        grid=(nbs, heads, ns // tq),
        in_specs=[
            pl.BlockSpec((tq, dq), lambda b, h, i: ((mp + b * ns) // tq + i, h)),
            pl.BlockSpec((ns, dq), lambda b, h, i: (mp // ns + b, h)),
            pl.BlockSpec((ns, dv), lambda b, h, i: (mp // ns + b, h)),
            pl.BlockSpec((past, dq), lambda b, h, i: (b, h)),
            pl.BlockSpec((past, dv), lambda b, h, i: (b, h)),
        ],
        out_specs=pl.BlockSpec((tq, dv), lambda b, h, i: (b * (ns // tq) + i, h)),
        out_shape=jax.ShapeDtypeStruct((nbs * ns, heads * dv), BF16),
        compiler_params=_params(("parallel", "parallel", "arbitrary")),
        name="attn_latent",
    )(q2d, k2d, v2d, kc2d, vc2d)


def _nat_key_row0(i, rows):
    return jnp.clip(NAT_Q_ROWS * i - NAT_WIN_ROWS // 2, 0, rows - NAT_K_ROWS)


def _nat_kernel(q_ref, k_ref, v_ref, kc_ref, vc_ref, b_ref, o_ref, *, rows):
    i = pl.program_id(2)
    start = pl.multiple_of(_nat_key_row0(i, rows) * GRID_W, GRID_W)
    win = pl.ds(start, NAT_K_ROWS * GRID_W)
    jobs = []
    for hd in range(NAT_HEADS_PER_STEP):
        cols = slice(hd * NAT_HD, (hd + 1) * NAT_HD)
        q = q_ref[:, cols].astype(BF16)
        s_win = _qkt(q, k_ref[win, cols].astype(BF16)) + b_ref[hd, 0]
        s_ctx = _qkt(q, kc_ref[:, cols].astype(BF16))
        jobs.append(([s_win, s_ctx], [v_ref[win, cols].astype(BF16), vc_ref[:, cols].astype(BF16)]))
    for hd, o in enumerate(_attend(jobs)):
        o_ref[:, hd * NAT_HD:(hd + 1) * NAT_HD] = o.astype(o_ref.dtype)


def _nat_bias_variants(rows):
    return (0, 1, rows // NAT_Q_ROWS - 1)


def _nat_bias(rpb, rows):
    w = GRID_W
    n_dr, n_dc = 2 * NAT_WIN_ROWS - 1, 2 * NAT_WIN_COLS - 1
    variants = _nat_bias_variants(rows)
    sel_r = np.zeros((len(variants), NAT_Q_ROWS, NAT_K_ROWS, n_dr), np.float32)
    for vi, i in enumerate(variants):
        row0 = min(max(NAT_Q_ROWS * i - NAT_WIN_ROWS // 2, 0), rows - NAT_K_ROWS)
        for a in range(NAT_Q_ROWS):
            qr = NAT_Q_ROWS * i + a
            r0 = min(max(qr - NAT_WIN_ROWS // 2, 0), rows - NAT_WIN_ROWS)
            for kl in range(NAT_K_ROWS):
                kr = row0 + kl
                if r0 <= kr < r0 + NAT_WIN_ROWS:
                    sel_r[vi, a, kl, kr - qr + NAT_WIN_ROWS - 1] = 1
    sel_c = np.zeros((n_dc, w, w), np.float32)
    for qc in range(w):
        c0 = min(max(qc - NAT_WIN_COLS // 2, 0), w - NAT_WIN_COLS)
        for kc in range(c0, c0 + NAT_WIN_COLS):
            sel_c[kc - qc + NAT_WIN_COLS - 1, qc, kc] = 1
    valid = (sel_r.sum(-1) > 0)[:, :, None, :, None] & (sel_c.sum(0) > 0)[None, None, :, None, :]
    rows_sel = jnp.einsum("hrd,vakr->hvakd", rpb.astype(F32), sel_r, precision=lax.Precision.HIGHEST)
    bias = jnp.einsum("hvakd,dqc->hvaqkc", rows_sel, sel_c, precision=lax.Precision.HIGHEST)
    bias = jnp.where(valid[None], bias * LOG2E, NEG_BIG)
    return bias.reshape(rpb.shape[0], len(variants), NAT_Q_ROWS * w, NAT_K_ROWS * w)


def nat_latent(y, kc2d, vc2d, bias, mp, ns, nbs, past):
    rows = ns // GRID_W
    tq = NAT_Q_ROWS * GRID_W
    nblk = ns // tq
    last = nblk - 1
    hps = NAT_HEADS_PER_STEP
    hg = NAT_HEADS // hps
    wide = hps * NAT_HD

    def bias_idx(b, h, i):
        return (h, jnp.where(i == 0, 0, jnp.where(i == last, 2, 1)), 0, 0)

    return pl.pallas_call(
        functools.partial(_nat_kernel, rows=rows),
        grid=(nbs, hg, nblk),
        in_specs=[
            pl.BlockSpec((tq, wide), lambda b, h, i: ((mp + b * ns) // tq + i, h)),
            pl.BlockSpec((ns, wide), lambda b, h, i: (mp // ns + b, hg + h)),
            pl.BlockSpec((ns, wide), lambda b, h, i: (mp // ns + b, 2 * hg + h)),
            pl.BlockSpec((past, wide), lambda b, h, i: (b, h)),
            pl.BlockSpec((past, wide), lambda b, h, i: (b, h)),
            pl.BlockSpec((hps, 1, tq, NAT_K_ROWS * GRID_W), bias_idx),
        ],
        out_specs=pl.BlockSpec((tq, wide), lambda b, h, i: (b * nblk + i, h)),
        out_shape=jax.ShapeDtypeStruct((nbs * ns, NAT_HEADS * NAT_HD), BF16),
        compiler_params=_params(("parallel", "parallel", "arbitrary")),
        name="nat_latent",
    )(y, y, y, kc2d, vc2d, bias)


def _hgrn_tables():
    t_ = HG_TILE
    nl = int(math.log2(t_))
    nm = HG_MXU_LEVELS
    a = np.zeros((2, (nm + 1) * t_, t_), np.float32)
    for lv in range(nm):
        w = 1 << lv
        for t in range(t_):
            mid = (t // (2 * w)) * 2 * w + w
            if t >= mid:
                a[0, lv * t_ + t, mid:t + 1] = 1
            else:
                a[0, lv * t_ + t, t + 1:mid] = 1
    for t in range(t_):
        a[0, nm * t_ + t, :t + 1] = 1
    for blk in range(nm + 1):
        a[1, blk * t_:(blk + 1) * t_] = a[0, blk * t_:(blk + 1) * t_][::-1, ::-1]
    lev = np.full((2, t_, t_), -1, np.int32)
    for t in range(t_):
        for s in range(t_):
            if s == t:
                lev[:, t, s] = nl
            elif s < t:
                lev[0, t, s] = (t ^ s).bit_length() - 1
            else:
                lev[1, t, s] = (t ^ s).bit_length() - 1
    return a, lev


def _hgrn_level_sums(cum, lv, d):
    w = 1 << lv
    parts = []
    for bs in range(0, HG_TILE, 2 * w):
        mid = bs + w
        brow = mid - 1 if d == 0 else mid
        b = cum[brow:brow + 1, :]
        lo, up = cum[bs:mid], cum[mid:bs + 2 * w]
        parts += [b - lo, up - b] if d == 0 else [lo - b, b - up]
    return jnp.concatenate(parts, axis=0)


def _hgrn_tiles(jobs, *, q_ref, f_refs, v_ref, lb_ref, a_ref, levs):
    t_ = HG_TILE
    nl = int(math.log2(t_))
    nm = HG_MXU_LEVELS
    acts = []
    for d, rows, cols, _ in jobs:
        lb = lb_ref[d:d + 1, cols]
        qr = q_ref[rows, cols]
        q = qr * _sigmoid(qr)
        f = lb + (1.0 - lb) * _sigmoid(f_refs[d][rows, cols])
        g = jnp.log(f)
        g_hi = g.astype(BF16)
        g_lo = (g - g_hi.astype(F32)).astype(BF16)
        acts.append((q, 1.0 - f, v_ref[rows, cols].astype(BF16), jnp.concatenate([g_hi, g_lo], axis=1)))
    sums = []
    for (d, _, _, _), (_, _, _, g2) in zip(jobs, acts):
        dd = jnp.dot(a_ref[d], g2, preferred_element_type=F32)
        sums.append(dd[:, :HG_K] + dd[:, HG_K:])
    atts = [jnp.where(levs[d] == nl, _qkt(q.astype(BF16), k.astype(BF16)), 0.0)
            for (d, _, _, _), (q, k, _, _) in zip(jobs, acts)]
    for lv in range(nl):
        for i, ((d, _, _, _), (q, k, _, _), dsum) in enumerate(zip(jobs, acts, sums)):
            cum = dsum[nm * t_:]
            e = jnp.exp(dsum[lv * t_:(lv + 1) * t_] if lv < nm else _hgrn_level_sums(cum, lv, d))
            atts[i] = jnp.where(levs[d] == lv, _qkt((q * e).astype(BF16), (k * e).astype(BF16)), atts[i])
    outs = []
    for (d, _, _, st), (q, k, v, _), dsum, att in zip(jobs, acts, sums, atts):
        if isinstance(st, int):
            st = outs[st][1]
        cum = dsum[nm * t_:]
        o = jnp.dot(att.astype(BF16), v, preferred_element_type=F32)
        o = o + _qkt((q * jnp.exp(cum)).astype(BF16), st.astype(BF16))
        tot_row = t_ - 1 if d == 0 else 0
        total = cum[tot_row:tot_row + 1]
        k_out = (k * jnp.exp(total - cum)).astype(BF16)
        st = st * jnp.exp(total) + lax.dot_general(v, k_out, (((0,), (0,)), ((), ())), preferred_element_type=F32)
        outs.append((o, st))
    return outs


def _hgrn_kernel(q_ref, f0_ref, f1_ref, v_ref, gate_ref, lb_ref, go_ref, a_ref, lev_ref, s0_ref,
                 o_ref, st_ref, o_scr, *, n, has_state):
    t_ = HG_TILE
    nt = n // t_
    levs = (lev_ref[0], lev_ref[1])
    hps = q_ref.shape[1] // HG_K
    chains = [(hd, d) for hd in range(hps) for d in range(2)]
    tiles = functools.partial(_hgrn_tiles, q_ref=q_ref, f_refs=(f0_ref, f1_ref), v_ref=v_ref, lb_ref=lb_ref,
                              a_ref=a_ref, levs=levs)
    if has_state:
        st0 = tuple(s0_ref[0, d, hd].T for hd, d in chains)
    else:
        st0 = tuple(jnp.zeros((HG_V, HG_K), F32) for _ in chains)

    def finish(o, rows, cols):
        o = o * lax.rsqrt(jnp.mean(o * o, axis=-1, keepdims=True) + EPS) * go_ref[...]
        gate = gate_ref[rows, cols]
        o_ref[rows, cols] = (o * (gate * _sigmoid(gate))).astype(o_ref.dtype)

    if nt <= HG_FLAT_TILES:
        jobs, last = [], {}
        for t in range(nt):
            for c, (hd, d) in enumerate(chains):
                tile = t if d == 0 else nt - 1 - t
                st = last.get(c, st0[c])
                last[c] = len(jobs)
                jobs.append((d, slice(tile * t_, (tile + 1) * t_), slice(hd * HG_K, (hd + 1) * HG_K), st))
        outs = tiles(jobs)
        for c, (hd, d) in enumerate(chains):
            st_ref[0, d, hd] = outs[last[c]][1].T
        sums = {}
        for (_, rows, cols, _), (o, _) in zip(jobs, outs):
            key = (rows.start, cols.start)
            sums[key] = (rows, cols, sums[key][2] + o) if key in sums else (rows, cols, o)
        for rows, cols, o in sums.values():
            finish(o, rows, cols)
        return

    o_scr[...] = jnp.zeros_like(o_scr)

    def tile_group(ti, sts):
        jobs = []
        for (hd, d), st in zip(chains, sts):
            tile = ti if d == 0 else nt - 1 - ti
            jobs.append((d, pl.ds(pl.multiple_of(tile * t_, t_), t_), slice(hd * HG_K, (hd + 1) * HG_K), st))
        outs = tiles(jobs)
        for (_, rows, cols, _), (o, _) in zip(jobs, outs):
            o_scr[rows, cols] += o
        return tuple(st for _, st in outs)

    sts = lax.fori_loop(0, nt, tile_group, st0, unroll=HG_UNROLL)
    for (hd, d), st in zip(chains, sts):
        st_ref[0, d, hd] = st.T
    for hd in range(hps):
        cols = slice(hd * HG_V, (hd + 1) * HG_V)
        finish(o_scr[:, cols], slice(None), cols)


def hgrn_scan(y, lb, g_o, s0, row_blk0, nb, n, tables, *, hps):
    a, lev = tables
    hh = HG_HEADS // hps
    has_state = s0 is not None
    if s0 is None:
        s0 = jnp.zeros((1, 2, hps, HG_K, HG_V), F32)
        s0_spec = pl.BlockSpec((1, 2, hps, HG_K, HG_V), lambda b, h: (0, 0, 0, 0, 0))
    else:
        s0_spec = pl.BlockSpec((1, 2, hps, HG_K, HG_V), lambda b, h: (b, 0, h, 0, 0))
    col = lambda c: pl.BlockSpec((n, hps * HG_K), lambda b, h: (row_blk0 + b, c * hh + h))
    return pl.pallas_call(
        functools.partial(_hgrn_kernel, n=n, has_state=has_state),
        grid=(nb, hh),
        in_specs=[
            col(0), col(1), col(2), col(3), col(4),
            pl.BlockSpec((2, hps * HG_K), lambda b, h: (0, h)),
            pl.BlockSpec((1, HG_V), lambda b, h: (0, 0)),
            pl.BlockSpec(a.shape, lambda b, h: (0, 0, 0)),
            pl.BlockSpec(lev.shape, lambda b, h: (0, 0, 0)),
            s0_spec,
        ],
        out_specs=[
            pl.BlockSpec((n, hps * HG_V), lambda b, h: (b, h)),
            pl.BlockSpec((1, 2, hps, HG_K, HG_V), lambda b, h: (b, 0, h, 0, 0)),
        ],
        out_shape=[
            jax.ShapeDtypeStruct((nb * n, HG_HEADS * HG_V), BF16),
            jax.ShapeDtypeStruct((nb, 2, HG_HEADS, HG_K, HG_V), F32),
        ],
        scratch_shapes=[pltpu.VMEM((n, hps * HG_V), F32)],
        compiler_params=_params(("parallel", "arbitrary")),
        name="hgrn_scan",
    )(y, y, y, y, y, lb, g_o.reshape(1, HG_V), a, lev, s0)


def kernel(x_prompt, x_sample, cache_mla_ckv, cache_mla_kpe, state_hgrn, cache_nat_k, cache_nat_v, c, c_ctx, w_mod, b_mod, g_norm_mix, g_norm_ffn, w_mla_qa, g_mla_qa, w_mla_qb, g_mla_qn, w_mla_kva, g_mla_kva, w_mla_kvb, g_mla_kn, w_mla_o, w_hg_q, w_hg_f, hg_lower_bounds, w_hg_i, w_hg_g, g_hg_o, w_hg_o, w_nat_qkv, g_nat_q, g_nat_k, nat_rpb, w_nat_o, w_ff1, w_ff2):
    bp, seq, d = x_prompt.shape
    nbs, ns, _ = x_sample.shape
    past = cache_mla_ckv.shape[2]
    depth = w_mod.shape[0]
    mp, ms = bp * seq, nbs * ns
    tm = 512
    tm_proj = 1024

    x = (x_prompt.reshape(mp, d), x_sample.reshape(ms, d))
    cvec = jnp.zeros((8, d), F32).at[0].set(c_ctx).at[1:1 + nbs].set(c)
    mods = adaln_all(cvec, w_mod, b_mod)

    lb_all = jnp.cumsum(jax.nn.softmax(hg_lower_bounds.astype(F32), axis=1), axis=1)
    lb_all = lb_all - lb_all[:, :1]
    rope_tabs = _rope_tables(tm, ns)
    hg_tabs = _hgrn_tables()
    hg_tabs = (jnp.asarray(hg_tabs[0], BF16), jnp.asarray(hg_tabs[1]))

    w_ff1_bf, w_ff2_bf = w_ff1.astype(BF16), w_ff2.astype(BF16)

    new_ckv, new_kpe, new_hg, new_nat_k, new_nat_v = [], [], [], [], []
    for i in range(depth):
        kind, j = i % 3, i // 3
        modrows = mods[i].reshape(8 * N_MOD, 1, d)
        if kind == 0:
            w1 = jnp.concatenate(
                [_swap_tail(w_mla_kva[j]), jnp.zeros((d, MLA_Q_LORA - MLA_KV_LORA - 2 * MLA_ROPE), F32), w_mla_qa[j]],
                axis=1).astype(BF16)
            y1 = nm_matmul(x, g_norm_mix[i], modrows, 0, 1, w1, mp, ns, tm=tm_proj, tn=MLA_Q_LORA)
            w_qb_pad = _swap_tail(w_mla_qb[j].reshape(MLA_Q_LORA, MLA_HEADS, MLA_QK)).reshape(
                MLA_Q_LORA, MLA_HEADS * MLA_QK_PAD).astype(BF16)
            g_qn_pad = _swap_tail(g_mla_qn[j].reshape(1, MLA_QK)) * (MLA_QK ** -0.5 * LOG2E)
            g_kn_pad = _swap_tail(g_mla_kn[j].reshape(1, MLA_QK))
            w_kvb = w_mla_kvb[j].astype(BF16)
            q2d = mla_q(y1, g_mla_qa[j], w_qb_pad, g_qn_pad, rope_tabs, mp, ns, tm=tm)
            ckv, k2d, v2d = mla_kv(y1, 0, y1, MLA_KV_LORA // LANES, g_mla_kva[j], w_kvb, g_kn_pad, rope_tabs,
                                   mp, ns, tm=tm, normalize=True)
            ckv_c = cache_mla_ckv[:, j].reshape(nbs * past, MLA_KV_LORA)
            kpe_c = _swap_tail(cache_mla_kpe[:, j].reshape(nbs * past, MLA_ROPE))
            ident_tabs = tuple(t[:past] for t in _rope_tables(past, ns))
            _, kc2d, vc2d = mla_kv(ckv_c, 0, kpe_c, 0, g_mla_kva[j], w_kvb, g_kn_pad, ident_tabs,
                                   nbs * past, ns, tm=past, normalize=False)
            o_p = attn_block(q2d, 0, k2d, 0, v2d, 0, bp, seq, heads=MLA_HEADS, dq=MLA_QK_PAD, dv=MLA_V)
            o_s = attn_latent(q2d, k2d, v2d, kc2d, vc2d, mp, ns, nbs, past, heads=MLA_HEADS, dq=MLA_QK_PAD,
                              dv=MLA_V, tq=1024)
            w_o = w_mla_o[j]
            new_ckv.append(ckv[:mp].reshape(bp, seq, MLA_KV_LORA))
            new_kpe.append(y1[:mp, MLA_KV_LORA:MLA_KV_LORA + MLA_ROPE].reshape(bp, seq, MLA_ROPE))
        elif kind == 1:
            w_all = jnp.concatenate([w_hg_q[j], w_hg_f[j, 0], w_hg_f[j, 1], w_hg_i[j], w_hg_g[j]], axis=1).astype(BF16)
            y = nm_matmul(x, g_norm_mix[i], modrows, 0, 1, w_all, mp, ns, tm=tm_proj, tn=1024)
            lb = lb_all[:, i]
            o_p, s_ctx = hgrn_scan(y, lb, g_hg_o[j], None, 0, bp, seq, hg_tabs, hps=4)
            o_s, _ = hgrn_scan(y, lb, g_hg_o[j], state_hgrn[:, j], mp // ns, nbs, ns, hg_tabs, hps=2)
            w_o = w_hg_o[j]
            new_hg.append(s_ctx)
        else:
            gains = jnp.concatenate([jnp.tile(g_nat_q[j] * (NAT_HD ** -0.5 * LOG2E), NAT_HEADS), jnp.tile(g_nat_k[j], NAT_HEADS),
                                     jnp.ones((d,), F32)]).reshape(1, 3 * d)
            y = nm_matmul(x, g_norm_mix[i], modrows, 0, 1, w_nat_qkv[j].astype(BF16), mp, ns, tm=tm_proj, tn=1024,
                          gains=gains, n_norm_cols=2 * d, head=NAT_HD)
            o_p = attn_block(y, 0, y, 1, y, 2, bp, seq, heads=NAT_HEADS, dq=NAT_HD, dv=NAT_HD)
            bias = _nat_bias(nat_rpb[j], ns // GRID_W)
            kc2d = cache_nat_k[:, j].reshape(nbs * past, d)
            vc2d = cache_nat_v[:, j].reshape(nbs * past, d)
            o_s = nat_latent(y, kc2d, vc2d, bias, mp, ns, nbs, past)
            w_o = w_nat_o[j]
            new_nat_k.append(y[:mp, d:2 * d].reshape(bp, seq, NAT_HEADS, NAT_HD))
            new_nat_v.append(y[:mp, 2 * d:].reshape(bp, seq, NAT_HEADS, NAT_HD))
        x = matmul_res(o_p, o_s, w_o.astype(BF16), x, modrows, 2, mp, ns, tm=tm)
        mlp = functools.partial(fused_mlp, x, g_norm_ffn[i], modrows, w_ff1_bf, w_ff2_bf, i, mp, ns, tm=tm, tf=1024)
        if i < depth - 1:
            x = mlp()
    return (mlp(row0=0, n_rows=mp).reshape(bp, seq, d), mlp(row0=mp, n_rows=ms).reshape(nbs, ns, d),
            jnp.stack(new_ckv, axis=1), jnp.stack(new_kpe, axis=1), jnp.stack(new_hg, axis=1),
            jnp.stack(new_nat_k, axis=1), jnp.stack(new_nat_v, axis=1))
```

```python
import functools
import math

import jax
import jax.numpy as jnp
import numpy as np
from jax import lax
from jax.experimental import pallas as pl
from jax.experimental.pallas import tpu as pltpu

F32 = jnp.float32
BF16 = jnp.bfloat16
EPS = 1e-6
ROPE_THETA = 10000.0
V7X_VMEM_LIMIT_BYTES = 56 * 1024 * 1024
LANES = 128
ROW_CHUNK = 256
ATTN_HEAD_GROUP = 4
ATTN_ROW_CHUNK = 256

N_MOD = 6
GRID_W = 64
MLA_HEADS, MLA_NOPE, MLA_ROPE, MLA_V = 16, 128, 64, 128
MLA_QK = MLA_NOPE + MLA_ROPE
MLA_QK_PAD = 256
MLA_Q_LORA, MLA_KV_LORA = 768, 512
HG_HEADS, HG_K, HG_V = 16, 128, 128
HG_TILE = 128
HG_FLAT_TILES = 2
HG_TILES_PER_ITER = 2
HG_MXU_LEVELS = 3
NAT_HEADS, NAT_HD = 16, 128
NAT_WIN_ROWS, NAT_WIN_COLS = 8, 16
NAT_HEADS_PER_STEP = 4
NAT_Q_ROWS = 4
NAT_K_ROWS = 12
NEG_BIG = -1e30
LOG2E = math.log2(math.e)


def _params(sem):
    return pltpu.CompilerParams(dimension_semantics=sem, vmem_limit_bytes=V7X_VMEM_LIMIT_BYTES)


def _mod_row(i, tm, mp, ns):
    start = i * tm
    return jnp.where(start < mp, 0, 1 + (start - mp) // ns)


def _sigmoid(x):
    return 0.5 * jnp.tanh(0.5 * x) + 0.5


def _norm_modulate(x, g, shift, scale):
    y = x * lax.rsqrt(jnp.mean(x * x, axis=-1, keepdims=True) + EPS) * g
    return y * (1.0 + scale) + shift


def _adaln_kernel(c_ref, w_ref, b_ref, o_ref):
    c = c_ref[...]
    s = (c * jax.nn.sigmoid(c)).astype(BF16)
    o_ref[0] = jnp.dot(s, w_ref[0].astype(BF16), preferred_element_type=F32) + b_ref[0]


def adaln_all(cvec, w_mod, b_mod):
    depth, d, n = w_mod.shape
    rows = cvec.shape[0]
    tn = 1024
    return pl.pallas_call(
        _adaln_kernel,
        grid=(depth, n // tn),
        in_specs=[
            pl.BlockSpec((rows, d), lambda l, j: (0, 0)),
            pl.BlockSpec((1, d, tn), lambda l, j: (l, 0, j)),
            pl.BlockSpec((1, 1, tn), lambda l, j: (l, 0, j)),
        ],
        out_specs=pl.BlockSpec((1, rows, tn), lambda l, j: (l, 0, j)),
        out_shape=jax.ShapeDtypeStruct((depth, rows, n), F32),
        compiler_params=_params(("parallel", "parallel")),
        name="adaln",
    )(cvec, w_mod, b_mod.reshape(depth, 1, n))


def _by_token_group(i, n_prompt_tiles, refs, body):
    if len(refs) == 1:
        body(refs[0])
    else:
        pl.when(i < n_prompt_tiles)(lambda: body(refs[0]))
        pl.when(i >= n_prompt_tiles)(lambda: body(refs[1]))


def _token_group_specs(block, n_prompt_tiles, n_src):
    if n_src == 1:
        return [pl.BlockSpec(block, lambda i, *_: (i, 0))]
    return [pl.BlockSpec(block, lambda i, *_: (jnp.minimum(i, n_prompt_tiles - 1), 0)),
            pl.BlockSpec(block, lambda i, *_: (jnp.maximum(i - n_prompt_tiles, 0), 0))]


def _nm_matmul_kernel(*refs, n_x, n_prompt_tiles, n_norm_tiles, head):
    x_refs = refs[:n_x]
    g_ref, sh_ref, sc_ref, w_ref, gain_ref, o_ref, h_scr = refs[n_x:]
    i, j = pl.program_id(0), pl.program_id(1)
    tm = h_scr.shape[0]

    def emit(rows, h, normed):
        acc = jnp.dot(h, w_ref[...], preferred_element_type=F32)
        if not normed:
            o_ref[rows, :] = acc
            return
        for c in range(acc.shape[1] // head):
            cols = slice(c * head, (c + 1) * head)
            blk = acc[:, cols]
            r = lax.rsqrt(jnp.mean(blk * blk, axis=-1, keepdims=True) + EPS)
            o_ref[rows, cols] = blk * r * gain_ref[:, cols]

    def first(x_ref):
        for r0 in range(0, tm, ROW_CHUNK):
            rows = slice(r0, r0 + ROW_CHUNK)
            h = _norm_modulate(x_ref[rows, :], g_ref[...], sh_ref[0], sc_ref[0]).astype(BF16)
            h_scr[rows, :] = h
            emit(rows, h, n_norm_tiles > 0)

    pl.when(j == 0)(lambda: _by_token_group(i, n_prompt_tiles, x_refs, first))
    if n_norm_tiles > 1:
        pl.when((j > 0) & (j < n_norm_tiles))(lambda: emit(slice(None), h_scr[...], True))
    pl.when(j >= max(n_norm_tiles, 1))(lambda: emit(slice(None), h_scr[...], False))


def nm_matmul(xs, g, modrows, k_shift, k_scale, w, mp, ns, *, tm, tn, gains=None, n_norm_cols=0, head=LANES):
    xs = xs if isinstance(xs, tuple) else (xs,)
    m, d = sum(x.shape[0] for x in xs), xs[0].shape[1]
    n = w.shape[1]
    if gains is None:
        gains = jnp.ones((1, n), F32)
    row = functools.partial(_mod_row, tm=tm, mp=mp, ns=ns)
    return pl.pallas_call(
        functools.partial(_nm_matmul_kernel, n_x=len(xs), n_prompt_tiles=mp // tm, n_norm_tiles=n_norm_cols // tn,
                          head=head),
        grid=(m // tm, n // tn),
        in_specs=_token_group_specs((tm, d), mp // tm, len(xs)) + [
            pl.BlockSpec((1, d), lambda i, j: (0, 0)),
            pl.BlockSpec((1, 1, d), lambda i, j: (row(i) * N_MOD + k_shift, 0, 0)),
            pl.BlockSpec((1, 1, d), lambda i, j: (row(i) * N_MOD + k_scale, 0, 0)),
            pl.BlockSpec((d, tn), lambda i, j: (0, j)),
            pl.BlockSpec((1, tn), lambda i, j: (0, j)),
        ],
        out_specs=pl.BlockSpec((tm, tn), lambda i, j: (i, j)),
        out_shape=jax.ShapeDtypeStruct((m, n), F32),
        scratch_shapes=[pltpu.VMEM((tm, d), BF16)],
        compiler_params=_params(("parallel", "arbitrary")),
        name="nm_matmul",
    )(*xs, g.reshape(1, d), modrows, modrows, w, gains)


def _matmul_res_kernel(*refs, n_x, n_prompt_tiles):
    ap_ref, as_ref, w_ref = refs[:3]
    x_refs = refs[3:3 + n_x]
    gt_ref, o_ref = refs[3 + n_x:]

    def finish(a_ref, x_ref):
        acc = jnp.dot(a_ref[...], w_ref[...], preferred_element_type=F32)
        o_ref[...] = x_ref[...] + gt_ref[0] * acc

    i = pl.program_id(0)
    pl.when(i < n_prompt_tiles)(lambda: finish(ap_ref, x_refs[0]))
    pl.when(i >= n_prompt_tiles)(lambda: finish(as_ref, x_refs[-1]))


def matmul_res(a_p, a_s, w, xs, modrows, k_gate, mp, ns, *, tm):
    xs = xs if isinstance(xs, tuple) else (xs,)
    kdim = a_p.shape[1]
    m = a_p.shape[0] + a_s.shape[0]
    n = w.shape[1]
    npt = mp // tm
    row = functools.partial(_mod_row, tm=tm, mp=mp, ns=ns)
    return pl.pallas_call(
        functools.partial(_matmul_res_kernel, n_x=len(xs), n_prompt_tiles=npt),
        grid=(m // tm,),
        in_specs=_token_group_specs((tm, kdim), npt, 2) + [pl.BlockSpec((kdim, n), lambda i: (0, 0))]
        + _token_group_specs((tm, n), npt, len(xs))
        + [pl.BlockSpec((1, 1, n), lambda i: (row(i) * N_MOD + k_gate, 0, 0))],
        out_specs=pl.BlockSpec((tm, n), lambda i: (i, 0)),
        out_shape=jax.ShapeDtypeStruct((m, n), F32),
        compiler_params=_params(("parallel",)),
        name="matmul_res",
    )(a_p, a_s, w, *xs, modrows)


def _mlp_kernel(x_ref, g_ref, sh_ref, sc_ref, gt_ref, w1_ref, w2_ref, o_ref, h_scr, acc_scr):
    f = pl.program_id(1)
    tm = x_ref.shape[0]

    def hidden(h):
        u = jnp.dot(h, w1_ref[...], preferred_element_type=F32)
        u = jnp.square(jnp.maximum(u, 0.0)).astype(BF16)
        return jnp.dot(u, w2_ref[...], preferred_element_type=F32)

    @pl.when(f == 0)
    def _():
        for r0 in range(0, tm, ROW_CHUNK):
            rows = slice(r0, r0 + ROW_CHUNK)
            h = _norm_modulate(x_ref[rows, :], g_ref[...], sh_ref[0], sc_ref[0]).astype(BF16)
            h_scr[rows, :] = h
            acc_scr[rows, :] = hidden(h)

    @pl.when(f > 0)
    def _():
        acc_scr[...] += hidden(h_scr[...])

    @pl.when(f == pl.num_programs(1) - 1)
    def _():
        o_ref[...] = x_ref[...] + gt_ref[0] * acc_scr[...]


def fused_mlp(x, g, modrows, w1, w2, layer, mp, ns, *, tm, tf, row0=0, n_rows=None):
    d = x.shape[1]
    m = x.shape[0] - row0 if n_rows is None else n_rows
    t0 = row0 // tm
    ff = w1.shape[2]
    row = functools.partial(_mod_row, tm=tm, mp=mp, ns=ns)
    modspec = lambda k: pl.BlockSpec((1, 1, d), lambda i, f: (row(i + t0) * N_MOD + k, 0, 0))
    return pl.pallas_call(
        _mlp_kernel,
        grid=(m // tm, ff // tf),
        in_specs=[
            pl.BlockSpec((tm, d), lambda i, f: (i + t0, 0)),
            pl.BlockSpec((1, d), lambda i, f: (0, 0)),
            modspec(3), modspec(4), modspec(5),
            pl.BlockSpec((None, d, tf), lambda i, f: (layer, 0, f)),
            pl.BlockSpec((None, tf, d), lambda i, f: (layer, f, 0)),
        ],
        out_specs=pl.BlockSpec((tm, d), lambda i, f: (i, 0)),
        out_shape=jax.ShapeDtypeStruct((m, d), F32),
        scratch_shapes=[pltpu.VMEM((tm, d), BF16), pltpu.VMEM((tm, d), F32)],
        compiler_params=_params(("parallel", "arbitrary")),
        name="fused_mlp",
    )(x, g.reshape(1, d), modrows, modrows, modrows, w1, w2)


def _rope_lanes(x, c_ref, s_ref):
    return x * c_ref[...] + pltpu.roll(x, LANES // 2, 1) * s_ref[...]


def _swap_tail(x):
    half = MLA_ROPE // 2
    pe = x[..., -MLA_ROPE:]
    return jnp.concatenate([x, pe[..., half:], pe[..., :half]], axis=-1)


def _mla_q_kernel(qa_ref, gqa_ref, w_ref, gqn_ref, c_ref, s_ref, o_ref):
    qa = qa_ref[...]
    qa = (qa * lax.rsqrt(jnp.mean(qa * qa, axis=-1, keepdims=True) + EPS) * gqa_ref[...]).astype(BF16)
    for h in range(MLA_HEADS):
        q = jnp.dot(qa, w_ref[:, h * MLA_QK_PAD:(h + 1) * MLA_QK_PAD], preferred_element_type=F32)
        nope, pe = q[:, :LANES], q[:, LANES:]
        ss = jnp.sum(nope * nope + 0.5 * (pe * pe), axis=-1, keepdims=True)
        r = lax.rsqrt(ss / MLA_QK + EPS)
        o_ref[:, h * MLA_QK_PAD:h * MLA_QK_PAD + LANES] = (nope * r * gqn_ref[:, :LANES]).astype(BF16)
        o_ref[:, h * MLA_QK_PAD + LANES:(h + 1) * MLA_QK_PAD] = _rope_lanes(
            pe * r * gqn_ref[:, LANES:], c_ref, s_ref).astype(BF16)


def _rope_spec(tm, mp, ns):
    def idx(i):
        start = i * tm
        return (jnp.where(start < mp, 0, 1 + ((start - mp) % ns) // tm), 0)

    return pl.BlockSpec((tm, LANES), idx)


def mla_q(y1, g_qa, w_qb_pad, g_qn_pad, tables, mp, ns, *, tm):
    m = y1.shape[0]
    n = MLA_HEADS * MLA_QK_PAD
    rs = _rope_spec(tm, mp, ns)
    return pl.pallas_call(
        _mla_q_kernel,
        grid=(m // tm,),
        in_specs=[
            pl.BlockSpec((tm, MLA_Q_LORA), lambda i: (i, 1)),
            pl.BlockSpec((1, MLA_Q_LORA), lambda i: (0, 0)),
            pl.BlockSpec((MLA_Q_LORA, n), lambda i: (0, 0)),
            pl.BlockSpec((1, MLA_QK_PAD), lambda i: (0, 0)),
            rs, rs,
        ],
        out_specs=pl.BlockSpec((tm, n), lambda i: (i, 0)),
        out_shape=jax.ShapeDtypeStruct((m, n), BF16),
        compiler_params=_params(("parallel",)),
        name="mla_q",
    )(y1, g_qa.reshape(1, MLA_Q_LORA), w_qb_pad, g_qn_pad, *tables)


def _mla_kv_kernel(ckv_ref, kpe_ref, gkva_ref, w_ref, gkn_ref, c_ref, s_ref,
                   ckv_out_ref, k_ref, v_ref, *, normalize):
    ckv = ckv_ref[...]
    if normalize:
        ckv = ckv * lax.rsqrt(jnp.mean(ckv * ckv, axis=-1, keepdims=True) + EPS) * gkva_ref[...]
    ckv_out_ref[...] = ckv
    ckv_b = ckv.astype(BF16)
    kpe = kpe_ref[...]
    pe_ss = 0.5 * jnp.sum(kpe * kpe, axis=-1, keepdims=True)
    pe_rot = _rope_lanes(kpe * gkn_ref[:, LANES:], c_ref, s_ref)
    for h in range(MLA_HEADS):
        kv = jnp.dot(ckv_b, w_ref[:, h * 2 * LANES:(h + 1) * 2 * LANES], preferred_element_type=F32)
        nope = kv[:, :LANES]
        r = lax.rsqrt((jnp.sum(nope * nope, axis=-1, keepdims=True) + pe_ss) / MLA_QK + EPS)
        k_ref[:, h * MLA_QK_PAD:h * MLA_QK_PAD + LANES] = (nope * r * gkn_ref[:, :LANES]).astype(BF16)
        k_ref[:, h * MLA_QK_PAD + LANES:(h + 1) * MLA_QK_PAD] = (pe_rot * r).astype(BF16)
        v_ref[:, h * MLA_V:(h + 1) * MLA_V] = kv[:, LANES:].astype(BF16)


def mla_kv(ckv_src, ckv_col, kpe_src, kpe_col, g_kva, w_kvb, g_kn_pad, tables, mp, ns, *, tm, normalize):
    m = ckv_src.shape[0]
    rs = _rope_spec(tm, mp, ns)
    return pl.pallas_call(
        functools.partial(_mla_kv_kernel, normalize=normalize),
        grid=(m // tm,),
        in_specs=[
            pl.BlockSpec((tm, MLA_KV_LORA), lambda i: (i, ckv_col)),
            pl.BlockSpec((tm, LANES), lambda i: (i, kpe_col)),
            pl.BlockSpec((1, MLA_KV_LORA), lambda i: (0, 0)),
            pl.BlockSpec((MLA_KV_LORA, MLA_HEADS * 2 * LANES), lambda i: (0, 0)),
            pl.BlockSpec((1, MLA_QK_PAD), lambda i: (0, 0)),
            rs, rs,
        ],
        out_specs=[
            pl.BlockSpec((tm, MLA_KV_LORA), lambda i: (i, 0)),
            pl.BlockSpec((tm, MLA_HEADS * MLA_QK_PAD), lambda i: (i, 0)),
            pl.BlockSpec((tm, MLA_HEADS * MLA_V), lambda i: (i, 0)),
        ],
        out_shape=[
            jax.ShapeDtypeStruct((m, MLA_KV_LORA), F32),
            jax.ShapeDtypeStruct((m, MLA_HEADS * MLA_QK_PAD), BF16),
            jax.ShapeDtypeStruct((m, MLA_HEADS * MLA_V), BF16),
        ],
        compiler_params=_params(("parallel",)),
        name="mla_kv",
    )(ckv_src, kpe_src, g_kva.reshape(1, MLA_KV_LORA), w_kvb, g_kn_pad, *tables)


def _rope_tables(tm, ns):
    n_freq = MLA_ROPE // 4
    inv = 1.0 / (ROPE_THETA ** (jnp.arange(n_freq, dtype=F32) / n_freq))
    t = jnp.arange(ns)
    row = (t // GRID_W).astype(F32)
    col = (t % GRID_W).astype(F32)
    ang = jnp.concatenate([row[:, None] * inv, col[:, None] * inv], axis=-1)
    cos, sin = jnp.cos(ang), jnp.sin(ang)
    z = jnp.zeros((ns, LANES - MLA_ROPE), F32)
    c = jnp.concatenate([cos, cos, z], axis=-1)
    s = jnp.concatenate([-sin, sin, z], axis=-1)
    ident = jnp.concatenate([jnp.ones((tm, MLA_ROPE), F32), jnp.zeros((tm, LANES - MLA_ROPE), F32)], axis=-1)
    return jnp.concatenate([ident, c], axis=0), jnp.concatenate([jnp.zeros((tm, LANES), F32), s], axis=0)


def _softmax_parts(scores):
    m = scores[0].max(axis=-1, keepdims=True)
    for s in scores[1:]:
        m = jnp.maximum(m, s.max(axis=-1, keepdims=True))
    ps = [jnp.exp2(s - m) for s in scores]
    l = ps[0].sum(axis=-1, keepdims=True)
    for p in ps[1:]:
        l = l + p.sum(axis=-1, keepdims=True)
    return ps, l


def _attend(jobs):
    ms = []
    for scores, _ in jobs:
        m = scores[0].max(axis=-1, keepdims=True)
        for s in scores[1:]:
            m = jnp.maximum(m, s.max(axis=-1, keepdims=True))
        ms.append(m)
    pss = [[jnp.exp2(s - m) for s in scores] for (scores, _), m in zip(jobs, ms)]
    ls = []
    for ps in pss:
        l = ps[0].sum(axis=-1, keepdims=True)
        for p in ps[1:]:
            l = l + p.sum(axis=-1, keepdims=True)
        ls.append(l)
    outs = []
    for (_, values), ps, l in zip(jobs, pss, ls):
        o = jnp.dot(ps[0].astype(BF16), values[0], preferred_element_type=F32)
        for p, v in zip(ps[1:], values[1:]):
            o = o + jnp.dot(p.astype(BF16), v, preferred_element_type=F32)
        outs.append(o / l)
    return outs


def _qkt(q, k):
    return lax.dot_general(q, k, (((1,), (1,)), ((), ())), preferred_element_type=F32)


def _attn_block_kernel(q_ref, k_ref, v_ref, o_ref, *, heads, dq, dv):
    for h0 in range(0, heads, ATTN_HEAD_GROUP):
        hs = range(h0, h0 + ATTN_HEAD_GROUP)
        jobs = [([_qkt(q_ref[:, h * dq:(h + 1) * dq].astype(BF16), k_ref[:, h * dq:(h + 1) * dq].astype(BF16))],
                 [v_ref[:, h * dv:(h + 1) * dv].astype(BF16)]) for h in hs]
        for h, o in zip(hs, _attend(jobs)):
            o_ref[:, h * dv:(h + 1) * dv] = o.astype(o_ref.dtype)


def attn_block(q_src, q_col, k_src, k_col, v_src, v_col, nb, seq, *, heads, dq, dv):
    return pl.pallas_call(
        functools.partial(_attn_block_kernel, heads=heads, dq=dq, dv=dv),
        grid=(nb,),
        in_specs=[
            pl.BlockSpec((seq, heads * dq), lambda b: (b, q_col)),
            pl.BlockSpec((seq, heads * dq), lambda b: (b, k_col)),
            pl.BlockSpec((seq, heads * dv), lambda b: (b, v_col)),
        ],
        out_specs=pl.BlockSpec((seq, heads * dv), lambda b: (b, 0)),
        out_shape=jax.ShapeDtypeStruct((nb * seq, heads * dv), BF16),
        compiler_params=_params(("parallel",)),
        name="attn_block",
    )(q_src, k_src, v_src)


def _attn_latent_kernel(q_ref, k_ref, v_ref, kc_ref, vc_ref, o_ref):
    chunks = [slice(r0, r0 + ATTN_ROW_CHUNK) for r0 in range(0, q_ref.shape[0], ATTN_ROW_CHUNK)]
    jobs = [([_qkt(q_ref[rows, :], k_ref[...]), _qkt(q_ref[rows, :], kc_ref[...])], [v_ref[...], vc_ref[...]])
            for rows in chunks]
    for rows, o in zip(chunks, _attend(jobs)):
        o_ref[rows, :] = o.astype(o_ref.dtype)


def attn_latent(q2d, k2d, v2d, kc2d, vc2d, mp, ns, nbs, past, *, heads, dq, dv, tq):
    return pl.pallas_call(
        _attn_latent_kernel,
        grid=(nbs, heads, ns // tq),
        in_specs=[
            pl.BlockSpec((tq, dq), lambda b, h, i: ((mp + b * ns) // tq + i, h)),
            pl.BlockSpec((ns, dq), lambda b, h, i: (mp // ns + b, h)),
            pl.BlockSpec((ns, dv), lambda b, h, i: (mp // ns + b, h)),
            pl.BlockSpec((past, dq), lambda b, h, i: (b, h)),
            pl.BlockSpec((past, dv), lambda b, h, i: (b, h)),
        ],
        out_specs=pl.BlockSpec((tq, dv), lambda b, h, i: (b * (ns // tq) + i, h)),
        out_shape=jax.ShapeDtypeStruct((nbs * ns, heads * dv), BF16),
        compiler_params=_params(("parallel", "parallel", "arbitrary")),
        name="attn_latent",
    )(q2d, k2d, v2d, kc2d, vc2d)


def _nat_key_row0(i, rows):
    return jnp.clip(NAT_Q_ROWS * i - NAT_WIN_ROWS // 2, 0, rows - NAT_K_ROWS)


def _nat_kernel(q_ref, k_ref, v_ref, kc_ref, vc_ref, b_ref, o_ref, *, rows):
    i = pl.program_id(2)
    start = pl.multiple_of(_nat_key_row0(i, rows) * GRID_W, GRID_W)
    win = pl.ds(start, NAT_K_ROWS * GRID_W)
    jobs = []
    for hd in range(NAT_HEADS_PER_STEP):
        cols = slice(hd * NAT_HD, (hd + 1) * NAT_HD)
        q = q_ref[:, cols].astype(BF16)
        s_win = _qkt(q, k_ref[win, cols].astype(BF16)) + b_ref[hd, 0]
        s_ctx = _qkt(q, kc_ref[:, cols].astype(BF16))
        jobs.append(([s_win, s_ctx], [v_ref[win, cols].astype(BF16), vc_ref[:, cols].astype(BF16)]))
    for hd, o in enumerate(_attend(jobs)):
        o_ref[:, hd * NAT_HD:(hd + 1) * NAT_HD] = o.astype(o_ref.dtype)


def _nat_bias_variants(rows):
    return (0, 1, rows // NAT_Q_ROWS - 1)


def _nat_bias(rpb, rows):
    w = GRID_W
    n_dr, n_dc = 2 * NAT_WIN_ROWS - 1, 2 * NAT_WIN_COLS - 1
    variants = _nat_bias_variants(rows)
    sel_r = np.zeros((len(variants), NAT_Q_ROWS, NAT_K_ROWS, n_dr), np.float32)
    for vi, i in enumerate(variants):
        row0 = min(max(NAT_Q_ROWS * i - NAT_WIN_ROWS // 2, 0), rows - NAT_K_ROWS)
        for a in range(NAT_Q_ROWS):
            qr = NAT_Q_ROWS * i + a
            r0 = min(max(qr - NAT_WIN_ROWS // 2, 0), rows - NAT_WIN_ROWS)
            for kl in range(NAT_K_ROWS):
                kr = row0 + kl
                if r0 <= kr < r0 + NAT_WIN_ROWS:
                    sel_r[vi, a, kl, kr - qr + NAT_WIN_ROWS - 1] = 1
    sel_c = np.zeros((n_dc, w, w), np.float32)
    for qc in range(w):
        c0 = min(max(qc - NAT_WIN_COLS // 2, 0), w - NAT_WIN_COLS)
        for kc in range(c0, c0 + NAT_WIN_COLS):
            sel_c[kc - qc + NAT_WIN_COLS - 1, qc, kc] = 1
    valid = (sel_r.sum(-1) > 0)[:, :, None, :, None] & (sel_c.sum(0) > 0)[None, None, :, None, :]
    rows_sel = jnp.einsum("hrd,vakr->hvakd", rpb.astype(F32), sel_r, precision=lax.Precision.HIGHEST)
    bias = jnp.einsum("hvakd,dqc->hvaqkc", rows_sel, sel_c, precision=lax.Precision.HIGHEST)
    bias = jnp.where(valid[None], bias * LOG2E, NEG_BIG)
    return bias.reshape(rpb.shape[0], len(variants), NAT_Q_ROWS * w, NAT_K_ROWS * w)


def nat_latent(y, kc2d, vc2d, bias, mp, ns, nbs, past):
    rows = ns // GRID_W
    tq = NAT_Q_ROWS * GRID_W
    nblk = ns // tq
    last = nblk - 1
    hps = NAT_HEADS_PER_STEP
    hg = NAT_HEADS // hps
    wide = hps * NAT_HD

    def bias_idx(b, h, i):
        return (h, jnp.where(i == 0, 0, jnp.where(i == last, 2, 1)), 0, 0)

    return pl.pallas_call(
        functools.partial(_nat_kernel, rows=rows),
        grid=(nbs, hg, nblk),
        in_specs=[
            pl.BlockSpec((tq, wide), lambda b, h, i: ((mp + b * ns) // tq + i, h)),
            pl.BlockSpec((ns, wide), lambda b, h, i: (mp // ns + b, hg + h)),
            pl.BlockSpec((ns, wide), lambda b, h, i: (mp // ns + b, 2 * hg + h)),
            pl.BlockSpec((past, wide), lambda b, h, i: (b, h)),
            pl.BlockSpec((past, wide), lambda b, h, i: (b, h)),
            pl.BlockSpec((hps, 1, tq, NAT_K_ROWS * GRID_W), bias_idx),
        ],
        out_specs=pl.BlockSpec((tq, wide), lambda b, h, i: (b * nblk + i, h)),
        out_shape=jax.ShapeDtypeStruct((nbs * ns, NAT_HEADS * NAT_HD), BF16),
        compiler_params=_params(("parallel", "parallel", "arbitrary")),
        name="nat_latent",
    )(y, y, y, kc2d, vc2d, bias)


def _hgrn_tables():
    t_ = HG_TILE
    nl = int(math.log2(t_))
    nm = HG_MXU_LEVELS
    a = np.zeros((2, (nm + 1) * t_, t_), np.float32)
    for lv in range(nm):
        w = 1 << lv
        for t in range(t_):
            mid = (t // (2 * w)) * 2 * w + w
            if t >= mid:
                a[0, lv * t_ + t, mid:t + 1] = 1
            else:
                a[0, lv * t_ + t, t + 1:mid] = 1
    for t in range(t_):
        a[0, nm * t_ + t, :t + 1] = 1
    for blk in range(nm + 1):
        a[1, blk * t_:(blk + 1) * t_] = a[0, blk * t_:(blk + 1) * t_][::-1, ::-1]
    lev = np.full((2, t_, t_), -1, np.int32)
    for t in range(t_):
        for s in range(t_):
            if s == t:
                lev[:, t, s] = nl
            elif s < t:
                lev[0, t, s] = (t ^ s).bit_length() - 1
            else:
                lev[1, t, s] = (t ^ s).bit_length() - 1
    return a, lev


def _hgrn_level_sums(cum, lv, d):
    w = 1 << lv
    parts = []
    for bs in range(0, HG_TILE, 2 * w):
        mid = bs + w
        brow = mid - 1 if d == 0 else mid
        b = cum[brow:brow + 1, :]
        lo, up = cum[bs:mid], cum[mid:bs + 2 * w]
        parts += [b - lo, up - b] if d == 0 else [lo - b, b - up]
    return jnp.concatenate(parts, axis=0)


def _hgrn_tiles(jobs, *, q_ref, f_refs, v_ref, lb_ref, a_ref, levs):
    t_ = HG_TILE
    nl = int(math.log2(t_))
    nm = HG_MXU_LEVELS
    acts = []
    for d, rows, cols, _ in jobs:
        lb = lb_ref[d:d + 1, cols]
        qr = q_ref[rows, cols]
        q = qr * _sigmoid(qr)
        f = lb + (1.0 - lb) * _sigmoid(f_refs[d][rows, cols])
        g = jnp.log(f)
        g_hi = g.astype(BF16)
        g_lo = (g - g_hi.astype(F32)).astype(BF16)
        acts.append((q, 1.0 - f, v_ref[rows, cols].astype(BF16), jnp.concatenate([g_hi, g_lo], axis=1)))
    sums = []
    for (d, _, _, _), (_, _, _, g2) in zip(jobs, acts):
        dd = jnp.dot(a_ref[d], g2, preferred_element_type=F32)
        sums.append(dd[:, :HG_K] + dd[:, HG_K:])
    atts = [jnp.where(levs[d] == nl, _qkt(q.astype(BF16), k.astype(BF16)), 0.0)
            for (d, _, _, _), (q, k, _, _) in zip(jobs, acts)]
    for lv in range(nl):
        for i, ((d, _, _, _), (q, k, _, _), dsum) in enumerate(zip(jobs, acts, sums)):
            cum = dsum[nm * t_:]
            e = jnp.exp(dsum[lv * t_:(lv + 1) * t_] if lv < nm else _hgrn_level_sums(cum, lv, d))
            atts[i] = jnp.where(levs[d] == lv, _qkt((q * e).astype(BF16), (k * e).astype(BF16)), atts[i])
    outs = []
    for (d, _, _, st), (q, k, v, _), dsum, att in zip(jobs, acts, sums, atts):
        if isinstance(st, int):
            st = outs[st][1]
        cum = dsum[nm * t_:]
        o = jnp.dot(att.astype(BF16), v, preferred_element_type=F32)
        o = o + _qkt((q * jnp.exp(cum)).astype(BF16), st.astype(BF16))
        tot_row = t_ - 1 if d == 0 else 0
        total = cum[tot_row:tot_row + 1]
        k_out = (k * jnp.exp(total - cum)).astype(BF16)
        st = st * jnp.exp(total) + lax.dot_general(v, k_out, (((0,), (0,)), ((), ())), preferred_element_type=F32)
        outs.append((o, st))
    return outs


def _hgrn_kernel(q_ref, f0_ref, f1_ref, v_ref, gate_ref, lb_ref, go_ref, a_ref, lev_ref, s0_ref,
                 o_ref, st_ref, o_scr, *, n, has_state):
    t_ = HG_TILE
    nt = n // t_
    levs = (lev_ref[0], lev_ref[1])
    hps = q_ref.shape[1] // HG_K
    chains = [(hd, d) for hd in range(hps) for d in range(2)]
    tiles = functools.partial(_hgrn_tiles, q_ref=q_ref, f_refs=(f0_ref, f1_ref), v_ref=v_ref, lb_ref=lb_ref,
                              a_ref=a_ref, levs=levs)
    if has_state:
        st0 = tuple(s0_ref[0, d, hd].T for hd, d in chains)
    else:
        st0 = tuple(jnp.zeros((HG_V, HG_K), F32) for _ in chains)

    def finish(o, rows, cols):
        o = o * lax.rsqrt(jnp.mean(o * o, axis=-1, keepdims=True) + EPS) * go_ref[...]
        gate = gate_ref[rows, cols]
        o_ref[rows, cols] = (o * (gate * _sigmoid(gate))).astype(o_ref.dtype)

    if nt <= HG_FLAT_TILES:
        jobs, last = [], {}
        for t in range(nt):
            for c, (hd, d) in enumerate(chains):
                tile = t if d == 0 else nt - 1 - t
                st = last.get(c, st0[c])
                last[c] = len(jobs)
                jobs.append((d, slice(tile * t_, (tile + 1) * t_), slice(hd * HG_K, (hd + 1) * HG_K), st))
        outs = tiles(jobs)
        for c, (hd, d) in enumerate(chains):
            st_ref[0, d, hd] = outs[last[c]][1].T
        sums = {}
        for (_, rows, cols, _), (o, _) in zip(jobs, outs):
            key = (rows.start, cols.start)
            sums[key] = (rows, cols, sums[key][2] + o) if key in sums else (rows, cols, o)
        for rows, cols, o in sums.values():
            finish(o, rows, cols)
        return

    assert nt % HG_TILES_PER_ITER == 0, (nt, HG_TILES_PER_ITER)
    o_scr[...] = jnp.zeros_like(o_scr)

    def tile_group(ti, sts):
        jobs, last = [], {}
        for t in range(HG_TILES_PER_ITER):
            step = ti * HG_TILES_PER_ITER + t
            for c, (hd, d) in enumerate(chains):
                tile = step if d == 0 else nt - 1 - step
                st = last.get(c, sts[c])
                last[c] = len(jobs)
                jobs.append((d, pl.ds(pl.multiple_of(tile * t_, t_), t_), slice(hd * HG_K, (hd + 1) * HG_K), st))
        outs = tiles(jobs)
        for (_, rows, cols, _), (o, _) in zip(jobs, outs):
            o_scr[rows, cols] += o
        return tuple(outs[last[c]][1] for c in range(len(chains)))

    sts = lax.fori_loop(0, nt // HG_TILES_PER_ITER, tile_group, st0)
    for (hd, d), st in zip(chains, sts):
        st_ref[0, d, hd] = st.T
    for hd in range(hps):
        cols = slice(hd * HG_V, (hd + 1) * HG_V)
        finish(o_scr[:, cols], slice(None), cols)


def hgrn_scan(y, lb, g_o, s0, row_blk0, nb, n, tables, *, hps):
    a, lev = tables
    hh = HG_HEADS // hps
    has_state = s0 is not None
    if s0 is None:
        s0 = jnp.zeros((1, 2, hps, HG_K, HG_V), F32)
        s0_spec = pl.BlockSpec((1, 2, hps, HG_K, HG_V), lambda b, h: (0, 0, 0, 0, 0))
    else:
        s0_spec = pl.BlockSpec((1, 2, hps, HG_K, HG_V), lambda b, h: (b, 0, h, 0, 0))
    col = lambda c: pl.BlockSpec((n, hps * HG_K), lambda b, h: (row_blk0 + b, c * hh + h))
    return pl.pallas_call(
        functools.partial(_hgrn_kernel, n=n, has_state=has_state),
        grid=(nb, hh),
        in_specs=[
            col(0), col(1), col(2), col(3), col(4),
            pl.BlockSpec((2, hps * HG_K), lambda b, h: (0, h)),
            pl.BlockSpec((1, HG_V), lambda b, h: (0, 0)),
            pl.BlockSpec(a.shape, lambda b, h: (0, 0, 0)),
            pl.BlockSpec(lev.shape, lambda b, h: (0, 0, 0)),
            s0_spec,
        ],
        out_specs=[
            pl.BlockSpec((n, hps * HG_V), lambda b, h: (b, h)),
            pl.BlockSpec((1, 2, hps, HG_K, HG_V), lambda b, h: (b, 0, h, 0, 0)),
        ],
        out_shape=[
            jax.ShapeDtypeStruct((nb * n, HG_HEADS * HG_V), BF16),
            jax.ShapeDtypeStruct((nb, 2, HG_HEADS, HG_K, HG_V), F32),
        ],
        scratch_shapes=[pltpu.VMEM((n, hps * HG_V), F32)],
        compiler_params=_params(("parallel", "arbitrary")),
        name="hgrn_scan",
    )(y, y, y, y, y, lb, g_o.reshape(1, HG_V), a, lev, s0)


def kernel(x_prompt, x_sample, cache_mla_ckv, cache_mla_kpe, state_hgrn, cache_nat_k, cache_nat_v, c, c_ctx, w_mod, b_mod, g_norm_mix, g_norm_ffn, w_mla_qa, g_mla_qa, w_mla_qb, g_mla_qn, w_mla_kva, g_mla_kva, w_mla_kvb, g_mla_kn, w_mla_o, w_hg_q, w_hg_f, hg_lower_bounds, w_hg_i, w_hg_g, g_hg_o, w_hg_o, w_nat_qkv, g_nat_q, g_nat_k, nat_rpb, w_nat_o, w_ff1, w_ff2):
    bp, seq, d = x_prompt.shape
    nbs, ns, _ = x_sample.shape
    past = cache_mla_ckv.shape[2]
    depth = w_mod.shape[0]
    mp, ms = bp * seq, nbs * ns
    tm = 512
    tm_proj = 1024

    x = (x_prompt.reshape(mp, d), x_sample.reshape(ms, d))
    cvec = jnp.zeros((8, d), F32).at[0].set(c_ctx).at[1:1 + nbs].set(c)
    mods = adaln_all(cvec, w_mod, b_mod)

    lb_all = jnp.cumsum(jax.nn.softmax(hg_lower_bounds.astype(F32), axis=1), axis=1)
    lb_all = lb_all - lb_all[:, :1]
    rope_tabs = _rope_tables(tm, ns)
    hg_tabs = _hgrn_tables()
    hg_tabs = (jnp.asarray(hg_tabs[0], BF16), jnp.asarray(hg_tabs[1]))

    w_ff1_bf, w_ff2_bf = w_ff1.astype(BF16), w_ff2.astype(BF16)

    new_ckv, new_kpe, new_hg, new_nat_k, new_nat_v = [], [], [], [], []
    for i in range(depth):
        kind, j = i % 3, i // 3
        modrows = mods[i].reshape(8 * N_MOD, 1, d)
        if kind == 0:
            w1 = jnp.concatenate(
                [_swap_tail(w_mla_kva[j]), jnp.zeros((d, MLA_Q_LORA - MLA_KV_LORA - 2 * MLA_ROPE), F32), w_mla_qa[j]],
                axis=1).astype(BF16)
            y1 = nm_matmul(x, g_norm_mix[i], modrows, 0, 1, w1, mp, ns, tm=tm_proj, tn=MLA_Q_LORA)
            w_qb_pad = _swap_tail(w_mla_qb[j].reshape(MLA_Q_LORA, MLA_HEADS, MLA_QK)).reshape(
                MLA_Q_LORA, MLA_HEADS * MLA_QK_PAD).astype(BF16)
            g_qn_pad = _swap_tail(g_mla_qn[j].reshape(1, MLA_QK)) * (MLA_QK ** -0.5 * LOG2E)
            g_kn_pad = _swap_tail(g_mla_kn[j].reshape(1, MLA_QK))
            w_kvb = w_mla_kvb[j].astype(BF16)
            q2d = mla_q(y1, g_mla_qa[j], w_qb_pad, g_qn_pad, rope_tabs, mp, ns, tm=tm)
            ckv, k2d, v2d = mla_kv(y1, 0, y1, MLA_KV_LORA // LANES, g_mla_kva[j], w_kvb, g_kn_pad, rope_tabs,
                                   mp, ns, tm=tm, normalize=True)
            ckv_c = cache_mla_ckv[:, j].reshape(nbs * past, MLA_KV_LORA)
            kpe_c = _swap_tail(cache_mla_kpe[:, j].reshape(nbs * past, MLA_ROPE))
            ident_tabs = tuple(t[:past] for t in _rope_tables(past, ns))
            _, kc2d, vc2d = mla_kv(ckv_c, 0, kpe_c, 0, g_mla_kva[j], w_kvb, g_kn_pad, ident_tabs,
                                   nbs * past, ns, tm=past, normalize=False)
            o_p = attn_block(q2d, 0, k2d, 0, v2d, 0, bp, seq, heads=MLA_HEADS, dq=MLA_QK_PAD, dv=MLA_V)
            o_s = attn_latent(q2d, k2d, v2d, kc2d, vc2d, mp, ns, nbs, past, heads=MLA_HEADS, dq=MLA_QK_PAD,
                              dv=MLA_V, tq=1024)
            w_o = w_mla_o[j]
            new_ckv.append(ckv[:mp].reshape(bp, seq, MLA_KV_LORA))
            new_kpe.append(y1[:mp, MLA_KV_LORA:MLA_KV_LORA + MLA_ROPE].reshape(bp, seq, MLA_ROPE))
        elif kind == 1:
            w_all = jnp.concatenate([w_hg_q[j], w_hg_f[j, 0], w_hg_f[j, 1], w_hg_i[j], w_hg_g[j]], axis=1).astype(BF16)
            y = nm_matmul(x, g_norm_mix[i], modrows, 0, 1, w_all, mp, ns, tm=tm_proj, tn=1024)
            lb = lb_all[:, i]
            o_p, s_ctx = hgrn_scan(y, lb, g_hg_o[j], None, 0, bp, seq, hg_tabs, hps=4)
            o_s, _ = hgrn_scan(y, lb, g_hg_o[j], state_hgrn[:, j], mp // ns, nbs, ns, hg_tabs, hps=2)
            w_o = w_hg_o[j]
            new_hg.append(s_ctx)
        else:
            gains = jnp.concatenate([jnp.tile(g_nat_q[j] * (NAT_HD ** -0.5 * LOG2E), NAT_HEADS), jnp.tile(g_nat_k[j], NAT_HEADS),
                                     jnp.ones((d,), F32)]).reshape(1, 3 * d)
            y = nm_matmul(x, g_norm_mix[i], modrows, 0, 1, w_nat_qkv[j].astype(BF16), mp, ns, tm=tm_proj, tn=1024,
                          gains=gains, n_norm_cols=2 * d, head=NAT_HD)
            o_p = attn_block(y, 0, y, 1, y, 2, bp, seq, heads=NAT_HEADS, dq=NAT_HD, dv=NAT_HD)
            bias = _nat_bias(nat_rpb[j], ns // GRID_W)
            kc2d = cache_nat_k[:, j].reshape(nbs * past, d)
            vc2d = cache_nat_v[:, j].reshape(nbs * past, d)
            o_s = nat_latent(y, kc2d, vc2d, bias, mp, ns, nbs, past)
            w_o = w_nat_o[j]
            new_nat_k.append(y[:mp, d:2 * d].reshape(bp, seq, NAT_HEADS, NAT_HD))
            new_nat_v.append(y[:mp, 2 * d:].reshape(bp, seq, NAT_HEADS, NAT_HD))
        x = matmul_res(o_p, o_s, w_o.astype(BF16), x, modrows, 2, mp, ns, tm=tm)
        mlp = functools.partial(fused_mlp, x, g_norm_ffn[i], modrows, w_ff1_bf, w_ff2_bf, i, mp, ns, tm=tm, tf=1024)
        if i < depth - 1:
            x = mlp()
    return (mlp(row0=0, n_rows=mp).reshape(bp, seq, d), mlp(row0=mp, n_rows=ms).reshape(nbs, ns, d),
            jnp.stack(new_ckv, axis=1), jnp.stack(new_kpe, axis=1), jnp.stack(new_hg, axis=1),
            jnp.stack(new_nat_k, axis=1), jnp.stack(new_nat_v, axis=1))
```
